```python
import math
import jax, jax.numpy as jnp
from jax import lax
import numpy as np

D_MODEL = 2048
BATCH = 8
SEQ = 2048
DEPTH = 2

A_HEADS = 8
HEAD_DIM = 128
A_WIDTH = A_HEADS * HEAD_DIM
DILATED_BRANCHES = ((128, 1), (512, 4), (2048, 16))
REL_BUCKETS = 32
REL_MAX_DIST = 2048
CONV_CH = D_MODEL - A_WIDTH
CONV_WIDTH = 31
IN_EVEN = 3 * A_WIDTH + 2 * CONV_CH
MIX_EVEN = A_WIDTH + CONV_CH
GMLP_WIDTH = D_MODEL
CHUNK = 128
GMLP_GROUPS = 16
GMLP_GROUP_CH = GMLP_WIDTH // GMLP_GROUPS
FF_DENSE = 5632
N_EXPERTS = 8
TOP_K = 2
FF_EXPERT = 7168
EPS = 1e-6
N_EVEN = (DEPTH + 1) // 2
N_ODD = DEPTH // 2

kernel_name = 'hybrid_dilated_conformer_gmlp_moe'


def _rmsnorm(x, g):
    x32 = x.astype(jnp.float32)
    y = x32 * lax.rsqrt(jnp.mean(x32 * x32, axis=-1, keepdims=True) + EPS)
    return (y * g).astype(x.dtype)


def _layernorm(x, g, b):
    x32 = x.astype(jnp.float32)
    mu = jnp.mean(x32, axis=-1, keepdims=True)
    xc = x32 - mu
    y = xc * lax.rsqrt(jnp.mean(xc * xc, axis=-1, keepdims=True) + EPS)
    return (y * g + b).astype(x.dtype)


def _t5_bucket(dist):
    max_exact = REL_BUCKETS // 2
    d = jnp.maximum(dist, 0)
    log_ratio = jnp.log(jnp.maximum(d, 1).astype(jnp.float32) / max_exact) / math.log(REL_MAX_DIST / max_exact)
    large = max_exact + (log_ratio * (REL_BUCKETS - max_exact)).astype(jnp.int32)
    large = jnp.minimum(large, REL_BUCKETS - 1)
    return jnp.where(d < max_exact, d, large)


def _dilated_branch(q, k, v, rel_bias, window, dil):
    bsz, s, h, e = q.shape
    blk = window // dil
    n_pos = s // dil
    nb = -(-n_pos // blk)
    lp = nb * blk

    def split(t):
        t = t.reshape(bsz, n_pos, dil, h, e)
        t = jnp.pad(t, ((0, 0), (0, lp - n_pos), (0, 0), (0, 0), (0, 0)))
        return t.reshape(bsz, nb, blk, dil, h, e)

    def with_prev(t):
        prev = jnp.pad(t, ((0, 0), (1, 0), (0, 0), (0, 0), (0, 0), (0, 0)))[:, :-1]
        return jnp.concatenate([prev, t], axis=2)

    qb = split(q)
    kb = with_prev(split(k))
    vb = with_prev(split(v))
    logits = jnp.einsum('bnirhe,bnjrhe->bnrhij', qb, kb) * (e ** -0.5)
    qi = jnp.arange(blk)[:, None]
    kj = jnp.arange(2 * blk)[None, :]
    dm = qi + blk - kj
    bias = rel_bias[_t5_bucket(dm * dil)].astype(jnp.float32)
    logits = logits + jnp.transpose(bias, (2, 0, 1))
    in_band = (dm >= 0) & (dm <= blk)
    prev_ok = (jnp.arange(nb)[:, None, None] > 0) | (kj[None] >= blk)
    mask = (in_band[None] & prev_ok)[None, :, None, None]
    logits = jnp.where(mask, logits, -jnp.inf)
    lse = jax.nn.logsumexp(logits, axis=-1)
    p = jnp.exp(logits - lse[..., None])
    o = jnp.einsum('bnrhij,bnjrhe->bnirhe', p, vb)
    o = o.reshape(bsz, lp, dil, h, e)[:, :n_pos].reshape(bsz, s, h, e)
    lse = jnp.transpose(lse, (0, 1, 4, 2, 3)).reshape(bsz, lp, dil, h)[:, :n_pos].reshape(bsz, s, h)
    return o, lse


def _dilated_mixture(q, k, v, rel_bias):
    outs, lses = [], []
    for window, dil in DILATED_BRANCHES:
        o, l = _dilated_branch(q, k, v, rel_bias, window, dil)
        outs.append(o)
        lses.append(l)
    wts = jax.nn.softmax(jnp.stack(lses), axis=0)
    return jnp.sum(wts[..., None] * jnp.stack(outs), axis=0)


def _causal_depthwise_conv(x, w, b):
    kw, c = w.shape
    y = lax.conv_general_dilated(x, w[:, None, :].astype(x.dtype), window_strides=(1,),
                                 padding=[(kw - 1, 0)], dimension_numbers=('NWC', 'WIO', 'NWC'),
                                 feature_group_count=c)
    return y + b


def _even_mixer(h, w_in, q_g, k_g, rel_bias, conv_w, conv_b, cn_g, cn_b, w_out):
    bsz, s, _ = h.shape
    p = h @ w_in
    q, k, v, cv, cg = jnp.split(p, [A_WIDTH, 2 * A_WIDTH, 3 * A_WIDTH, 3 * A_WIDTH + CONV_CH], axis=-1)

    def heads(t):
        return t.reshape(bsz, s, A_HEADS, HEAD_DIM).astype(jnp.float32)

    q = _rmsnorm(heads(q), q_g)
    k = _rmsnorm(heads(k), k_g)
    o_a = _dilated_mixture(q, k, heads(v), rel_bias).reshape(bsz, s, A_WIDTH).astype(h.dtype)
    c = cv * jax.nn.sigmoid(cg)
    c = _causal_depthwise_conv(c, conv_w, conv_b)
    o_b = jax.nn.silu(_layernorm(c, cn_g, cn_b))
    return jnp.concatenate([o_a, o_b], axis=-1) @ w_out


def _gmlp_mixer(h, w_u, b_u, vn_g, vn_b, w_s, b_s, w_o):
    bsz, s, _ = h.shape
    z = jax.nn.gelu(h @ w_u + b_u)
    u, v = jnp.split(z, 2, axis=-1)
    v = _layernorm(v, vn_g, vn_b).reshape(bsz, s // CHUNK, CHUNK, GMLP_GROUPS, GMLP_GROUP_CH)
    causal = jnp.tril(jnp.ones((CHUNK, CHUNK), dtype=bool))
    ws = jnp.where(causal[None], w_s, jnp.zeros((), w_s.dtype)).astype(v.dtype)
    sv = jnp.einsum('gts,bcsgd->bctgd', ws, v) + b_s.T[:, :, None]
    return (u * sv.reshape(bsz, s, GMLP_WIDTH)) @ w_o


def _swiglu(h, w1, w3, w2):
    return (jax.nn.silu(h @ w1) * (h @ w3)) @ w2


def _moe_swiglu(h, w_router, w_gate, w_up, w_down):
    bsz, s, d = h.shape
    t = h.reshape(-1, d)
    n = t.shape[0]
    logits = t.astype(jnp.float32) @ w_router.astype(jnp.float32)
    top_val, top_idx = lax.top_k(logits, TOP_K)
    gates = jax.nn.softmax(top_val, axis=-1)
    expert = top_idx.reshape(-1)
    token = jnp.repeat(jnp.arange(n, dtype=jnp.int32), TOP_K)
    order = jnp.argsort(expert)
    tok_s = token[order]
    group_sizes = jnp.bincount(expert, length=N_EXPERTS).astype(jnp.int32)
    xs = t[tok_s]
    hid = jax.nn.silu(lax.ragged_dot(xs, w_gate, group_sizes)) * lax.ragged_dot(xs, w_up, group_sizes)
    ys = lax.ragged_dot(hid, w_down, group_sizes)
    ys = ys * gates.reshape(-1)[order][:, None].astype(ys.dtype)
    out = jnp.zeros_like(t).at[tok_s].add(ys)
    return out.reshape(bsz, s, d)


def setup_inputs(seed: int = 0) -> dict:
    key = jax.random.key(seed)
    ks = iter(jax.random.split(key, 32))

    def nrm(shape, scale):
        return jax.random.normal(next(ks), shape, jnp.float32) * scale

    def gain(shape):
        return 1.0 + nrm(shape, 0.02)

    return {
        'x': nrm((BATCH, SEQ, D_MODEL), 1.0),
        'rel_bias': nrm((REL_BUCKETS, A_HEADS), 0.1),
        'even_norm_mix': gain((N_EVEN, D_MODEL)),
        'even_w_in': nrm((N_EVEN, D_MODEL, IN_EVEN), D_MODEL ** -0.5),
        'even_q_norm': gain((N_EVEN, HEAD_DIM)),
        'even_k_norm': gain((N_EVEN, HEAD_DIM)),
        'even_conv_w': nrm((N_EVEN, CONV_WIDTH, CONV_CH), CONV_WIDTH ** -0.5),
        'even_conv_b': nrm((N_EVEN, CONV_CH), 0.02),
        'even_cnorm_g': gain((N_EVEN, CONV_CH)),
        'even_cnorm_b': nrm((N_EVEN, CONV_CH), 0.02),
        'even_w_out': nrm((N_EVEN, MIX_EVEN, D_MODEL), MIX_EVEN ** -0.5),
        'even_norm_ffn': gain((N_EVEN, D_MODEL)),
        'even_ffn_w1': nrm((N_EVEN, D_MODEL, FF_DENSE), D_MODEL ** -0.5),
        'even_ffn_w3': nrm((N_EVEN, D_MODEL, FF_DENSE), D_MODEL ** -0.5),
        'even_ffn_w2': nrm((N_EVEN, FF_DENSE, D_MODEL), FF_DENSE ** -0.5),
        'odd_norm_mix': gain((N_ODD, D_MODEL)),
        'odd_w_u': nrm((N_ODD, D_MODEL, 2 * GMLP_WIDTH), D_MODEL ** -0.5),
        'odd_b_u': nrm((N_ODD, 2 * GMLP_WIDTH), 0.02),
        'odd_vnorm_g': gain((N_ODD, GMLP_WIDTH)),
        'odd_vnorm_b': nrm((N_ODD, GMLP_WIDTH), 0.02),
        'odd_w_s': nrm((N_ODD, GMLP_GROUPS, CHUNK, CHUNK), CHUNK ** -0.5),
        'odd_b_s': nrm((N_ODD, GMLP_GROUPS, CHUNK), 0.02),
        'odd_w_o': nrm((N_ODD, GMLP_WIDTH, D_MODEL), GMLP_WIDTH ** -0.5),
        'odd_norm_ffn': gain((N_ODD, D_MODEL)),
        'odd_router': nrm((N_ODD, D_MODEL, N_EXPERTS), D_MODEL ** -0.5),
        'odd_we_gate': nrm((N_ODD, N_EXPERTS, D_MODEL, FF_EXPERT), D_MODEL ** -0.5),
        'odd_we_up': nrm((N_ODD, N_EXPERTS, D_MODEL, FF_EXPERT), D_MODEL ** -0.5),
        'odd_we_down': nrm((N_ODD, N_EXPERTS, FF_EXPERT, D_MODEL), FF_EXPERT ** -0.5),
    }


def reference(x, rel_bias, even_norm_mix, even_w_in, even_q_norm, even_k_norm, even_conv_w,
              even_conv_b, even_cnorm_g, even_cnorm_b, even_w_out, even_norm_ffn, even_ffn_w1,
              even_ffn_w3, even_ffn_w2, odd_norm_mix, odd_w_u, odd_b_u, odd_vnorm_g, odd_vnorm_b,
              odd_w_s, odd_b_s, odd_w_o, odd_norm_ffn, odd_router, odd_we_gate, odd_we_up,
              odd_we_down):
    for layer in range(DEPTH):
        i = layer // 2
        if layer % 2 == 0:
            x = x + _even_mixer(_rmsnorm(x, even_norm_mix[i]), even_w_in[i], even_q_norm[i],
                                even_k_norm[i], rel_bias, even_conv_w[i], even_conv_b[i],
                                even_cnorm_g[i], even_cnorm_b[i], even_w_out[i])
            x = x + _swiglu(_rmsnorm(x, even_norm_ffn[i]), even_ffn_w1[i], even_ffn_w3[i], even_ffn_w2[i])
        else:
            x = x + _gmlp_mixer(_rmsnorm(x, odd_norm_mix[i]), odd_w_u[i], odd_b_u[i], odd_vnorm_g[i],
                                odd_vnorm_b[i], odd_w_s[i], odd_b_s[i], odd_w_o[i])
            x = x + _moe_swiglu(_rmsnorm(x, odd_norm_ffn[i]), odd_router[i], odd_we_gate[i],
                                odd_we_up[i], odd_we_down[i])
    return x
```

```python
import functools
import math

import numpy as np
import jax
import jax.numpy as jnp
from jax import lax
from jax.experimental import pallas as pl
from jax.experimental.pallas import tpu as pltpu

F32 = jnp.float32
BF16 = jnp.bfloat16
EPS = 1e-6

HEAD_DIM = 128
A_HEADS = 8
DILATED_BRANCHES = ((128, 1), (512, 4), (2048, 16))
ATT_BLK = 128
REL_BUCKETS = 32
REL_MAX_DIST = 2048
CONV_WIDTH = 31
CONV_HALO = 32
CHUNK = 128
GMLP_GROUPS = 16
N_EXPERTS = 8
NEG = -1e30

VMEM_LIMIT_BYTES = 56 * 1024 * 1024
MOE_TM = 1024
MOE_TF = 256


def _cparams(sem):
    return pltpu.CompilerParams(dimension_semantics=sem, vmem_limit_bytes=VMEM_LIMIT_BYTES)


def _rms(x, g):
    ms = jnp.mean(x * x, axis=-1, keepdims=True)
    return x * lax.rsqrt(ms + EPS) * g


def _rmsnorm_kernel(x_ref, g_ref, o_ref):
    o_ref[...] = _rms(x_ref[...], g_ref[...]).astype(o_ref.dtype)


def _rmsnorm(x, g, tm=512):
    n, d = x.shape
    return pl.pallas_call(
        _rmsnorm_kernel,
        grid=(n // tm,),
        in_specs=[pl.BlockSpec((tm, d), lambda i: (i, 0)),
                  pl.BlockSpec((1, d), lambda i: (0, 0))],
        out_specs=pl.BlockSpec((tm, d), lambda i: (i, 0)),
        out_shape=jax.ShapeDtypeStruct((n, d), BF16),
        compiler_params=_cparams(("parallel",)),
        name="rmsnorm",
    )(x, g.reshape(1, d))


def _qkv_kernel(h_ref, w_ref, qg_ref, kg_ref, o_ref, *, tn, q_tiles, k_tiles, q_scale):
    j = pl.program_id(1)
    acc = jnp.dot(h_ref[...], w_ref[...], preferred_element_type=F32)

    def head_norm(g, scale):
        for hd in range(tn // HEAD_DIM):
            sl = slice(hd * HEAD_DIM, (hd + 1) * HEAD_DIM)
            o_ref[:, sl] = _rms(acc[:, sl], g) * scale

    @pl.when(j < q_tiles)
    def _():
        head_norm(qg_ref[...], q_scale)

    @pl.when(jnp.logical_and(j >= q_tiles, j < q_tiles + k_tiles))
    def _():
        head_norm(kg_ref[...], 1.0)

    @pl.when(j >= q_tiles + k_tiles)
    def _():
        o_ref[...] = acc


def _qkv_proj(h, w, q_g, k_g, a_width, tm=1024, tn=512):
    n, d = h.shape
    nout = w.shape[1]
    kern = functools.partial(_qkv_kernel, tn=tn, q_tiles=a_width // tn, k_tiles=a_width // tn,
                             q_scale=HEAD_DIM ** -0.5)
    return pl.pallas_call(
        kern,
        grid=(n // tm, nout // tn),
        in_specs=[pl.BlockSpec((tm, d), lambda i, j: (i, 0)),
                  pl.BlockSpec((d, tn), lambda i, j: (0, j)),
                  pl.BlockSpec((1, HEAD_DIM), lambda i, j: (0, 0)),
                  pl.BlockSpec((1, HEAD_DIM), lambda i, j: (0, 0))],
        out_specs=pl.BlockSpec((tm, tn), lambda i, j: (i, j)),
        out_shape=jax.ShapeDtypeStruct((n, nout), F32),
        compiler_params=_cparams(("parallel", "parallel")),
        name="qkv_proj",
    )(h, w, q_g.reshape(1, HEAD_DIM), k_g.reshape(1, HEAD_DIM))


def _gelu_tanh(x):
    c = math.sqrt(2.0 / math.pi)
    return 0.5 * x * (1.0 + jnp.tanh(c * (x + 0.044715 * (x * x * x))))


def _mm_kernel(h_ref, w_ref, o_ref):
    o_ref[...] = jnp.dot(h_ref[...], w_ref[...], preferred_element_type=F32).astype(o_ref.dtype)


def _mm_bias_gelu_kernel(h_ref, w_ref, b_ref, o_ref):
    acc = jnp.dot(h_ref[...], w_ref[...], preferred_element_type=F32) + b_ref[...]
    o_ref[...] = _gelu_tanh(acc).astype(o_ref.dtype)


def _matmul(h, w, bias=None, tm=1024, tn=512, name="matmul"):
    n, d = h.shape
    m = w.shape[1]
    in_specs = [pl.BlockSpec((tm, d), lambda i, j: (i, 0)),
                pl.BlockSpec((d, tn), lambda i, j: (0, j))]
    args = [h, w]
    kern = _mm_kernel
    if bias is not None:
        in_specs.append(pl.BlockSpec((1, tn), lambda i, j: (0, j)))
        args.append(bias.reshape(1, m))
        kern = _mm_bias_gelu_kernel
    return pl.pallas_call(
        kern,
        grid=(n // tm, m // tn),
        in_specs=in_specs,
        out_specs=pl.BlockSpec((tm, tn), lambda i, j: (i, j)),
        out_shape=jax.ShapeDtypeStruct((n, m), BF16),
        compiler_params=_cparams(("parallel", "parallel")),
        name=name,
    )(*args)


def _t5_bucket_np(dist):
    max_exact = REL_BUCKETS // 2
    d = np.maximum(dist, 0)
    log_ratio = (np.log(np.maximum(d, 1).astype(np.float32) / np.float32(max_exact))
                 / np.float32(math.log(REL_MAX_DIST / max_exact)))
    large = max_exact + (log_ratio.astype(np.float32) * np.float32(REL_BUCKETS - max_exact)).astype(np.int32)
    large = np.minimum(large, REL_BUCKETS - 1)
    return np.where(d < max_exact, d, large).astype(np.int32)


def _branch_bucket_tables():
    blk = ATT_BLK
    qi = np.arange(blk)[:, None]
    kj = np.arange(2 * blk)[None, :]
    dm = qi + blk - kj
    tabs = []
    for _, dil in DILATED_BRANCHES:
        bucket = _t5_bucket_np(dm * dil)
        tabs.append(np.where((dm >= 0) & (dm <= blk), bucket, -1))
    return np.stack(tabs).astype(np.int32)


def _attn_kernel(rb_ref, bidx_ref, q_ref, k_ref, v_ref, o_ref, bias_scr, ob_scr, m_scr, l_scr, *, seq):
    blk = ATT_BLK
    hidx = pl.program_id(0)

    @pl.when(pl.program_id(1) == 0)
    def _():
        for br in range(len(DILATED_BRANCHES)):
            idx = bidx_ref[br]
            bias = jnp.full(idx.shape, NEG, F32)
            for u in range(REL_BUCKETS):
                bias = jnp.where(idx == u, rb_ref[u, hidx], bias)
            bias_scr[br] = bias

    def block(br, q_rows, k_rows, bias):
        qb = q_ref[0, q_rows, :].astype(BF16)
        kb = k_ref[0, k_rows, :].astype(BF16)
        vb = v_ref[0, k_rows, :].astype(BF16)
        s = lax.dot_general(qb, kb, (((1,), (1,)), ((), ())), preferred_element_type=F32) + bias
        m = jnp.max(s, axis=-1, keepdims=True)
        p = jnp.exp(s - m)
        l = jnp.sum(p, axis=-1, keepdims=True)
        o = jnp.dot(p.astype(BF16), vb, preferred_element_type=F32)
        ob_scr[br, q_rows, :] = o
        m_scr[br, q_rows, :] = jnp.broadcast_to(m, (blk, HEAD_DIM))
        l_scr[br, q_rows, :] = jnp.broadcast_to(l, (blk, HEAD_DIM))

    for br, (window, dil) in enumerate(DILATED_BRANCHES):
        n_pos = seq // dil
        nb = -(-n_pos // blk)
        for r in range(dil):
            for nblk in range(nb):
                start = r + dil * blk * nblk

                def rows(first, count):
                    return pl.ds(first, count) if dil == 1 else pl.ds(first, count, stride=dil)

                q_rows = rows(start, blk)
                if nblk == 0:
                    block(br, q_rows, q_rows, bias_scr[br, :, blk:])
                else:
                    block(br, q_rows, rows(start - dil * blk, 2 * blk), bias_scr[br])

    n_br = len(DILATED_BRANCHES)
    m_all = m_scr[0]
    for br in range(1, n_br):
        m_all = jnp.maximum(m_all, m_scr[br])
    num = jnp.zeros((seq, HEAD_DIM), F32)
    den = jnp.zeros((seq, HEAD_DIM), F32)
    for br in range(n_br):
        a = jnp.exp(m_scr[br] - m_all)
        num = num + a * ob_scr[br]
        den = den + a * l_scr[br]
    o_ref[0] = (num / den).astype(o_ref.dtype)


def _attention(qkv, rel_bias, bsz, seq):
    h, e = A_HEADS, HEAD_DIM
    n_br = len(DILATED_BRANCHES)
    bidx = jnp.asarray(_branch_bucket_tables())
    kern = functools.partial(_attn_kernel, seq=seq)
    return pl.pallas_call(
        kern,
        grid=(h, bsz),
        in_specs=[pl.BlockSpec(memory_space=pltpu.SMEM),
                  pl.BlockSpec((n_br, ATT_BLK, 2 * ATT_BLK), lambda hh, b: (0, 0, 0)),
                  pl.BlockSpec((1, seq, e), lambda hh, b: (b, 0, hh)),
                  pl.BlockSpec((1, seq, e), lambda hh, b: (b, 0, h + hh)),
                  pl.BlockSpec((1, seq, e), lambda hh, b: (b, 0, 2 * h + hh))],
        out_specs=pl.BlockSpec((1, seq, e), lambda hh, b: (b, 0, hh)),
        out_shape=jax.ShapeDtypeStruct((bsz, seq, h * e), BF16),
        scratch_shapes=[pltpu.VMEM((n_br, ATT_BLK, 2 * ATT_BLK), F32),
                        pltpu.VMEM((n_br, seq, e), F32),
                        pltpu.VMEM((n_br, seq, e), F32),
                        pltpu.VMEM((n_br, seq, e), F32)],
        compiler_params=_cparams(("arbitrary", "arbitrary")),
        name="dilated_attention",
    )(rel_bias, bidx, qkv, qkv, qkv)


def _conv_kernel(cv_ref, cg_ref, hv_ref, hg_ref, w_ref, b_ref, lg_ref, lb_ref, o_ref, g_scr, y_scr, *, tt):
    halo = CONV_HALO
    t = pl.program_id(1)
    hv = hv_ref[0].astype(F32)
    hg = hg_ref[0].astype(F32)
    g_scr[0:halo, :] = jnp.where(t > 0, hv * jax.nn.sigmoid(hg), 0.0)
    cv = cv_ref[0].astype(F32)
    cg = cg_ref[0].astype(F32)
    g_scr[halo:, :] = cv * jax.nn.sigmoid(cg)

    ch = g_scr.shape[1]
    rc, cc = 32, 256
    first = halo - (CONV_WIDTH - 1)
    for r0 in range(0, tt, rc):
        for c0 in range(0, ch, cc):
            acc = jnp.broadcast_to(b_ref[:, c0:c0 + cc], (rc, cc))
            for k in range(CONV_WIDTH):
                acc = acc + w_ref[k:k + 1, c0:c0 + cc] * g_scr[r0 + first + k:r0 + first + k + rc, c0:c0 + cc]
            y_scr[r0:r0 + rc, c0:c0 + cc] = acc

    y = y_scr[...]
    mu = jnp.mean(y, axis=-1, keepdims=True)
    yc = y - mu
    var = jnp.mean(yc * yc, axis=-1, keepdims=True)
    z = yc * lax.rsqrt(var + EPS) * lg_ref[...] + lb_ref[...]
    o_ref[0] = (z * jax.nn.sigmoid(z)).astype(o_ref.dtype)


def _conv_module(pc, conv_w, conv_b, ln_g, ln_b, tt=256):
    bsz, seq, ch2 = pc.shape
    ch = ch2 // 2
    hb = tt // CONV_HALO
    kern = functools.partial(_conv_kernel, tt=tt)
    vec = lambda a: a.reshape(1, ch)
    return pl.pallas_call(
        kern,
        grid=(bsz, seq // tt),
        in_specs=[pl.BlockSpec((1, tt, ch), lambda b, t: (b, t, 0)),
                  pl.BlockSpec((1, tt, ch), lambda b, t: (b, t, 1)),
                  pl.BlockSpec((1, CONV_HALO, ch), lambda b, t: (b, jnp.maximum(t * hb - 1, 0), 0)),
                  pl.BlockSpec((1, CONV_HALO, ch), lambda b, t: (b, jnp.maximum(t * hb - 1, 0), 1)),
                  pl.BlockSpec((CONV_WIDTH, ch), lambda b, t: (0, 0)),
                  pl.BlockSpec((1, ch), lambda b, t: (0, 0)),
                  pl.BlockSpec((1, ch), lambda b, t: (0, 0)),
                  pl.BlockSpec((1, ch), lambda b, t: (0, 0))],
        out_specs=pl.BlockSpec((1, tt, ch), lambda b, t: (b, t, 0)),
        out_shape=jax.ShapeDtypeStruct((bsz, seq, ch), BF16),
        scratch_shapes=[pltpu.VMEM((tt + CONV_HALO, ch), F32), pltpu.VMEM((tt, ch), F32)],
        compiler_params=_cparams(("parallel", "parallel")),
        name="conv_module",
    )(pc, pc, pc, pc, conv_w, vec(conv_b), vec(ln_g), vec(ln_b))


def _outproj_kernel(a_ref, b_ref, wa_ref, wb_ref, x_ref, g_ref, xo_ref, ho_ref):
    acc = jnp.dot(a_ref[...], wa_ref[...], preferred_element_type=F32)
    acc = acc + jnp.dot(b_ref[...], wb_ref[...], preferred_element_type=F32)
    xn = x_ref[...] + acc
    xo_ref[...] = xn
    ho_ref[...] = _rms(xn, g_ref[...]).astype(ho_ref.dtype)


def _outproj(a, b, wa, wb, x, g, tm=512):
    n, d = x.shape
    ka, kb = a.shape[1], b.shape[1]
    return pl.pallas_call(
        _outproj_kernel,
        grid=(n // tm,),
        in_specs=[pl.BlockSpec((tm, ka), lambda i: (i, 0)),
                  pl.BlockSpec((tm, kb), lambda i: (i, 0)),
                  pl.BlockSpec((ka, d), lambda i: (0, 0)),
                  pl.BlockSpec((kb, d), lambda i: (0, 0)),
                  pl.BlockSpec((tm, d), lambda i: (i, 0)),
                  pl.BlockSpec((1, d), lambda i: (0, 0))],
        out_specs=[pl.BlockSpec((tm, d), lambda i: (i, 0)),
                   pl.BlockSpec((tm, d), lambda i: (i, 0))],
        out_shape=[jax.ShapeDtypeStruct((n, d), F32), jax.ShapeDtypeStruct((n, d), BF16)],
        compiler_params=_cparams(("parallel",)),
        name="out_proj",
    )(a, b, wa, wb, x, g.reshape(1, d))


def _ffn_kernel(h_ref, w1_ref, w3_ref, w2_ref, x_ref, g_ref, xo_ref, ho_ref):
    f = pl.program_id(1)
    h = h_ref[...]
    a = jnp.dot(h, w1_ref[...], preferred_element_type=F32)
    b = jnp.dot(h, w3_ref[...], preferred_element_type=F32)
    hid = (a * jax.nn.sigmoid(a) * b).astype(BF16)
    part = jnp.dot(hid, w2_ref[...], preferred_element_type=F32)

    @pl.when(f == 0)
    def _():
        xo_ref[...] = part

    @pl.when(f > 0)
    def _():
        xo_ref[...] += part

    @pl.when(f == pl.num_programs(1) - 1)
    def _():
        xn = x_ref[...] + xo_ref[...]
        xo_ref[...] = xn
        ho_ref[...] = _rms(xn, g_ref[...]).astype(ho_ref.dtype)


def _ffn(h, w1, w3, w2, x, g, tm=512, tf=512):
    n, d = x.shape
    ff = w1.shape[1]
    return pl.pallas_call(
        _ffn_kernel,
        grid=(n // tm, ff // tf),
        in_specs=[pl.BlockSpec((tm, d), lambda i, f: (i, 0)),
                  pl.BlockSpec((d, tf), lambda i, f: (0, f)),
                  pl.BlockSpec((d, tf), lambda i, f: (0, f)),
                  pl.BlockSpec((tf, d), lambda i, f: (f, 0)),
                  pl.BlockSpec((tm, d), lambda i, f: (i, 0)),
                  pl.BlockSpec((1, d), lambda i, f: (0, 0))],
        out_specs=[pl.BlockSpec((tm, d), lambda i, f: (i, 0)),
                   pl.BlockSpec((tm, d), lambda i, f: (i, 0))],
        out_shape=[jax.ShapeDtypeStruct((n, d), F32), jax.ShapeDtypeStruct((n, d), BF16)],
        compiler_params=_cparams(("parallel", "arbitrary")),
        name="dense_swiglu",
    )(h, w1, w3, w2, x, g.reshape(1, d))


def _gate_kernel(z_ref, ws_ref, bs_ref, vg_ref, vb_ref, wo_ref, x_ref, g_ref, wr_ref,
                 xo_ref, ho_ref, lg_ref, gated_scr, *, tm, width):
    gch = width // GMLP_GROUPS
    row = lax.broadcasted_iota(jnp.int32, (CHUNK, CHUNK), 0)
    col = lax.broadcasted_iota(jnp.int32, (CHUNK, CHUNK), 1)
    causal = row >= col
    for c0 in range(0, tm, CHUNK):
        v = z_ref[c0:c0 + CHUNK, width:].astype(F32)
        mu = jnp.mean(v, axis=-1, keepdims=True)
        vc = v - mu
        var = jnp.mean(vc * vc, axis=-1, keepdims=True)
        vn = (vc * lax.rsqrt(var + EPS) * vg_ref[...] + vb_ref[...]).astype(BF16)
        for gi in range(GMLP_GROUPS):
            cs = slice(gi * gch, (gi + 1) * gch)
            wsg = jnp.where(causal, ws_ref[gi], 0.0).astype(BF16)
            sv = jnp.dot(wsg, vn[:, cs], preferred_element_type=F32) + bs_ref[:, cs]
            u = z_ref[c0:c0 + CHUNK, cs].astype(F32)
            gated_scr[c0:c0 + CHUNK, cs] = (u * sv).astype(BF16)
    xn = x_ref[...] + jnp.dot(gated_scr[...], wo_ref[...], preferred_element_type=F32)
    xo_ref[...] = xn
    hn = _rms(xn, g_ref[...])
    ho_ref[...] = hn.astype(ho_ref.dtype)
    lg_ref[...] = jnp.dot(hn, wr_ref[...], preferred_element_type=F32, precision=lax.Precision.HIGHEST)


def _gate_outproj(z, w_s, b_s, vn_g, vn_b, w_o, x, g, w_router, tm=512):
    n, d = x.shape
    width = z.shape[1] // 2
    gch = width // GMLP_GROUPS
    bs_x = jnp.repeat(b_s.T, gch, axis=1)
    wr = jnp.zeros((d, 128), F32).at[:, :N_EXPERTS].set(w_router)
    kern = functools.partial(_gate_kernel, tm=tm, width=width)
    full = lambda shape: pl.BlockSpec(shape, lambda i: (0,) * len(shape))
    return pl.pallas_call(
        kern,
        grid=(n // tm,),
        in_specs=[pl.BlockSpec((tm, 2 * width), lambda i: (i, 0)),
                  full((GMLP_GROUPS, CHUNK, CHUNK)),
                  full((CHUNK, width)),
                  full((1, width)),
                  full((1, width)),
                  full((width, d)),
                  pl.BlockSpec((tm, d), lambda i: (i, 0)),
                  full((1, d)),
                  full((d, 128))],
        out_specs=[pl.BlockSpec((tm, d), lambda i: (i, 0)),
                   pl.BlockSpec((tm, d), lambda i: (i, 0)),
                   pl.BlockSpec((tm, 128), lambda i: (i, 0))],
        out_shape=[jax.ShapeDtypeStruct((n, d), F32), jax.ShapeDtypeStruct((n, d), BF16),
                   jax.ShapeDtypeStruct((n, 128), F32)],
        scratch_shapes=[pltpu.VMEM((tm, width), BF16)],
        compiler_params=_cparams(("parallel",)),
        name="gmlp_gate_outproj",
    )(z, w_s, bs_x, vn_g.reshape(1, width), vn_b.reshape(1, width), w_o, x, g.reshape(1, d), wr)


def _router_kernel(lg_ref, oi_ref, og_ref, cnt_ref, carry_scr, *, tm):
    i = pl.program_id(0)

    @pl.when(i == 0)
    def _():
        carry_scr[...] = jnp.zeros_like(carry_scr)

    lane = lax.broadcasted_iota(jnp.int32, (tm, 128), 1)
    lg = jnp.where(lane < N_EXPERTS, lg_ref[...], -jnp.inf)
    m1 = jnp.max(lg, axis=-1, keepdims=True)
    i1 = jnp.min(jnp.where(lg == m1, lane, 128), axis=-1, keepdims=True)
    lg2 = jnp.where(lane == i1, -jnp.inf, lg)
    m2 = jnp.max(lg2, axis=-1, keepdims=True)
    i2 = jnp.min(jnp.where(lg2 == m2, lane, 128), axis=-1, keepdims=True)
    e2 = jnp.exp(m2 - m1)
    den = 1.0 + e2
    g1 = 1.0 / den
    g2 = e2 / den

    sel = jnp.logical_or(lane == i1, lane == i2)
    row = lax.broadcasted_iota(jnp.int32, (tm, tm), 0)
    col = lax.broadcasted_iota(jnp.int32, (tm, tm), 1)
    before = jnp.where(row > col, 1.0, 0.0).astype(BF16)
    selb = jnp.where(sel, 1.0, 0.0)
    tot = jnp.dot(before, selb.astype(BF16), preferred_element_type=F32) + carry_scr[0:1, :]
    r1 = jnp.sum(jnp.where(lane == i1, tot, 0.0), axis=-1, keepdims=True).astype(jnp.int32)
    r2 = jnp.sum(jnp.where(lane == i2, tot, 0.0), axis=-1, keepdims=True).astype(jnp.int32)
    new_carry = carry_scr[0:1, :] + jnp.sum(selb, axis=0, keepdims=True)
    carry_scr[...] = jnp.broadcast_to(new_carry, carry_scr.shape)
    cnt_ref[...] = jnp.broadcast_to(new_carry, cnt_ref.shape)

    oi_ref[...] = jnp.where(lane == 0, i1, jnp.where(lane == 1, i2, jnp.where(lane == 2, r1, r2)))
    og_ref[...] = jnp.where(lane == 0, g1, g2)


def _router(logits, tm=512):
    n = logits.shape[0]
    kern = functools.partial(_router_kernel, tm=tm)
    return pl.pallas_call(
        kern,
        grid=(n // tm,),
        in_specs=[pl.BlockSpec((tm, 128), lambda i: (i, 0))],
        out_specs=[pl.BlockSpec((tm, 128), lambda i: (i, 0)),
                   pl.BlockSpec((tm, 128), lambda i: (i, 0)),
                   pl.BlockSpec((8, 128), lambda i: (0, 0))],
        out_shape=[jax.ShapeDtypeStruct((n, 128), jnp.int32), jax.ShapeDtypeStruct((n, 128), F32),
                   jax.ShapeDtypeStruct((8, 128), F32)],
        scratch_shapes=[pltpu.VMEM((8, 128), F32)],
        compiler_params=_cparams(("arbitrary",)),
        name="router_top2",
    )(logits)


def _dispatch_kernel(p1_ref, p2_ref, h_ref, xs_in_ref, xs_ref, sem, *, tt):
    del xs_in_ref
    base = pl.program_id(0) * tt

    def copies(t):
        src = h_ref.at[pl.ds(base + t, 1)]
        return (pltpu.make_async_copy(src, xs_ref.at[pl.ds(p1_ref[base + t], 1)], sem),
                pltpu.make_async_copy(src, xs_ref.at[pl.ds(p2_ref[base + t], 1)], sem))

    def start(t, carry):
        for c in copies(t):
            c.start()
        return carry

    def wait(t, carry):
        for c in copies(t):
            c.wait()
        return carry

    lax.fori_loop(0, tt, start, 0)
    lax.fori_loop(0, tt, wait, 0)


def _dispatch(h, pos1, pos2, n_rows, tt=512):
    n, d = h.shape
    lanes = 128
    h3 = h.reshape(n, d // lanes, lanes)
    xs0 = jnp.zeros((n_rows, d // lanes, lanes), h.dtype)
    kern = functools.partial(_dispatch_kernel, tt=tt)
    xs = pl.pallas_call(
        kern,
        grid_spec=pltpu.PrefetchScalarGridSpec(
            num_scalar_prefetch=2,
            grid=(n // tt,),
            in_specs=[pl.BlockSpec(memory_space=pl.ANY), pl.BlockSpec(memory_space=pl.ANY)],
            out_specs=pl.BlockSpec(memory_space=pl.ANY),
            scratch_shapes=[pltpu.SemaphoreType.DMA(())]),
        out_shape=jax.ShapeDtypeStruct(xs0.shape, xs0.dtype),
        input_output_aliases={3: 0},
        compiler_params=_cparams(("arbitrary",)),
        name="moe_dispatch",
    )(pos1, pos2, h3, xs0)
    return xs.reshape(n_rows, d)


def _moe_kernel(te_ref, nt_ref, xs_ref, wg_ref, wu_ref, wd_ref, ys_ref):
    i = pl.program_id(0)
    f = pl.program_id(1)

    @pl.when(i < nt_ref[0])
    def _():
        xs = xs_ref[...]
        a = jnp.dot(xs, wg_ref[0].astype(BF16), preferred_element_type=F32)
        b = jnp.dot(xs, wu_ref[0].astype(BF16), preferred_element_type=F32)
        hid = (a * jax.nn.sigmoid(a) * b).astype(BF16)
        part = jnp.dot(hid, wd_ref[0].astype(BF16), preferred_element_type=F32)

        @pl.when(f == 0)
        def _():
            ys_ref[...] = part

        @pl.when(f > 0)
        def _():
            ys_ref[...] += part

    @pl.when(jnp.logical_and(i >= nt_ref[0], f == 0))
    def _():
        ys_ref[...] = jnp.zeros_like(ys_ref)


def _moe_experts(xs, tile_expert, n_live, w_gate, w_up, w_down, tm=MOE_TM, tf=MOE_TF):
    n_rows, d = xs.shape
    ff = w_gate.shape[2]
    n_tiles = n_rows // tm
    n_f = ff // tf

    def live_tile(i, nt):
        return jnp.minimum(i, nt[0] - 1)

    def live_f(i, f, nt):
        return jnp.where(i < nt[0], f, n_f - 1)

    return pl.pallas_call(
        _moe_kernel,
        grid_spec=pltpu.PrefetchScalarGridSpec(
            num_scalar_prefetch=2,
            grid=(n_tiles, n_f),
            in_specs=[pl.BlockSpec((tm, d), lambda i, f, te, nt: (live_tile(i, nt), 0)),
                      pl.BlockSpec((1, d, tf), lambda i, f, te, nt: (te[i], 0, live_f(i, f, nt))),
                      pl.BlockSpec((1, d, tf), lambda i, f, te, nt: (te[i], 0, live_f(i, f, nt))),
                      pl.BlockSpec((1, tf, d), lambda i, f, te, nt: (te[i], live_f(i, f, nt), 0))],
            out_specs=pl.BlockSpec((tm, d), lambda i, f, te, nt: (i, 0))),
        out_shape=jax.ShapeDtypeStruct((n_rows, d), F32),
        compiler_params=_cparams(("arbitrary", "arbitrary")),
        name="moe_experts",
    )(tile_expert, n_live, xs, w_gate, w_up, w_down)


def _combine_kernel(p1_ref, p2_ref, ys_ref, x_ref, g_ref, o_ref, buf1, buf2, sem, *, tt):
    base = pl.program_id(0) * tt

    def copies(t):
        return (pltpu.make_async_copy(ys_ref.at[pl.ds(p1_ref[base + t], 1)], buf1.at[pl.ds(t, 1)], sem),
                pltpu.make_async_copy(ys_ref.at[pl.ds(p2_ref[base + t], 1)], buf2.at[pl.ds(t, 1)], sem))

    def start(t, carry):
        for c in copies(t):
            c.start()
        return carry

    def wait(t, carry):
        for c in copies(t):
            c.wait()
        return carry

    lax.fori_loop(0, tt, start, 0)
    lax.fori_loop(0, tt, wait, 0)
    g = g_ref[...]
    o_ref[...] = x_ref[...] + (buf1[...] * g[:, 0:1] + buf2[...] * g[:, 1:2])


def _combine(ys, pos1, pos2, gates, x, tt=256):
    n, d = x.shape
    kern = functools.partial(_combine_kernel, tt=tt)
    return pl.pallas_call(
        kern,
        grid_spec=pltpu.PrefetchScalarGridSpec(
            num_scalar_prefetch=2,
            grid=(n // tt,),
            in_specs=[pl.BlockSpec(memory_space=pl.ANY),
                      pl.BlockSpec((tt, d), lambda i, p1, p2: (i, 0)),
                      pl.BlockSpec((tt, 128), lambda i, p1, p2: (i, 0))],
            out_specs=pl.BlockSpec((tt, d), lambda i, p1, p2: (i, 0)),
            scratch_shapes=[pltpu.VMEM((tt, d), F32), pltpu.VMEM((tt, d), F32),
                            pltpu.SemaphoreType.DMA(())]),
        out_shape=jax.ShapeDtypeStruct((n, d), F32),
        compiler_params=_cparams(("arbitrary",)),
        name="moe_combine",
    )(pos1, pos2, ys, x, gates)


def _even_layer(x, rel_bias, norm_mix, w_in, q_g, k_g, conv_w, conv_b, cn_g, cn_b, w_out, norm_next,
                bsz, seq):
    n, d = x.shape
    a_width = A_HEADS * HEAD_DIM
    h = _rmsnorm(x, norm_mix)
    w_in = w_in.astype(BF16)
    qkv = _qkv_proj(h, w_in[:, :3 * a_width], q_g, k_g, a_width)
    pc = _matmul(h, w_in[:, 3 * a_width:], name="conv_proj")
    ch = pc.shape[1] // 2
    o_a = _attention(qkv.reshape(bsz, seq, 3 * a_width), rel_bias, bsz, seq).reshape(n, a_width)
    o_b = _conv_module(pc.reshape(bsz, seq, 2 * ch), conv_w, conv_b, cn_g, cn_b).reshape(n, ch)
    w_out = w_out.astype(BF16)
    return _outproj(o_a, o_b, w_out[:a_width], w_out[a_width:], x, norm_next)


def _moe_layer(x, h, logits, w_gate, w_up, w_down):
    n, d = x.shape
    tm = MOE_TM
    oi, og, cnt = _router(logits)
    e1, e2, r1, r2 = oi[:, 0], oi[:, 1], oi[:, 2], oi[:, 3]
    counts = cnt[0, :N_EXPERTS].astype(jnp.int32)
    padded = ((counts + tm - 1) // tm) * tm
    ends = jnp.cumsum(padded)
    starts = ends - padded
    pos1 = starts[e1] + r1
    pos2 = starts[e2] + r2
    n_tiles = (2 * n) // tm + N_EXPERTS
    n_live = (ends[-1] // tm).astype(jnp.int32).reshape(1)
    tile_row = jnp.minimum(jnp.arange(n_tiles, dtype=jnp.int32), n_live[0] - 1) * tm
    tile_expert = jnp.sum(tile_row[:, None] >= ends[None, :], axis=1).astype(jnp.int32)
    xs = _dispatch(h, pos1, pos2, n_tiles * tm)
    ys = _moe_experts(xs, tile_expert, n_live, w_gate, w_up, w_down)
    return _combine(ys, pos1, pos2, og, x)


def kernel(x, rel_bias, even_norm_mix, even_w_in, even_q_norm, even_k_norm, even_conv_w, even_conv_b, even_cnorm_g, even_cnorm_b, even_w_out, even_norm_ffn, even_ffn_w1, even_ffn_w3, even_ffn_w2, odd_norm_mix, odd_w_u, odd_b_u, odd_vnorm_g, odd_vnorm_b, odd_w_s, odd_b_s, odd_w_o, odd_norm_ffn, odd_router, odd_we_gate, odd_we_up, odd_we_down):
    bsz, seq, d = x.shape
    xf = x.reshape(bsz * seq, d)
    xf, h = _even_layer(xf, rel_bias, even_norm_mix[0], even_w_in[0], even_q_norm[0], even_k_norm[0],
                        even_conv_w[0], even_conv_b[0], even_cnorm_g[0], even_cnorm_b[0], even_w_out[0],
                        even_norm_ffn[0], bsz, seq)
    xf, h = _ffn(h, even_ffn_w1[0].astype(BF16), even_ffn_w3[0].astype(BF16), even_ffn_w2[0].astype(BF16),
                 xf, odd_norm_mix[0])
    z = _matmul(h, odd_w_u[0].astype(BF16), bias=odd_b_u[0], name="gmlp_in_proj")
    xf, h, logits = _gate_outproj(z, odd_w_s[0], odd_b_s[0], odd_vnorm_g[0], odd_vnorm_b[0],
                                  odd_w_o[0].astype(BF16), xf, odd_norm_ffn[0], odd_router[0])
    xf = _moe_layer(xf, h, logits, odd_we_gate[0], odd_we_up[0], odd_we_down[0])
    return xf.reshape(bsz, seq, d)
```

```python
import functools
import math

import numpy as np
import jax
import jax.numpy as jnp
from jax import lax
from jax.experimental import pallas as pl
from jax.experimental.pallas import tpu as pltpu

F32 = jnp.float32
BF16 = jnp.bfloat16
EPS = 1e-6

HEAD_DIM = 128
A_HEADS = 8
DILATED_BRANCHES = ((128, 1), (512, 4), (2048, 16))
ATT_BLK = 128
REL_BUCKETS = 32
REL_MAX_DIST = 2048
CONV_WIDTH = 31
CONV_HALO = 32
CHUNK = 128
GMLP_GROUPS = 16
N_EXPERTS = 8
NEG = -1e30

VMEM_LIMIT_BYTES = 56 * 1024 * 1024
MOE_TM = 1024
MOE_TF = 512


def _cparams(sem):
    return pltpu.CompilerParams(dimension_semantics=sem, vmem_limit_bytes=VMEM_LIMIT_BYTES)


def _rms(x, g):
    ms = jnp.mean(x * x, axis=-1, keepdims=True)
    return x * lax.rsqrt(ms + EPS) * g


def _rmsnorm_kernel(x_ref, g_ref, o_ref):
    o_ref[...] = _rms(x_ref[...], g_ref[...]).astype(o_ref.dtype)


def _rmsnorm(x, g, tm=512):
    n, d = x.shape
    return pl.pallas_call(
        _rmsnorm_kernel,
        grid=(n // tm,),
        in_specs=[pl.BlockSpec((tm, d), lambda i: (i, 0)),
                  pl.BlockSpec((1, d), lambda i: (0, 0))],
        out_specs=pl.BlockSpec((tm, d), lambda i: (i, 0)),
        out_shape=jax.ShapeDtypeStruct((n, d), BF16),
        compiler_params=_cparams(("parallel",)),
        name="rmsnorm",
    )(x, g.reshape(1, d))


def _qkv_kernel(h_ref, w_ref, qg_ref, kg_ref, o_ref, *, tn, q_tiles, k_tiles, q_scale):
    j = pl.program_id(1)
    acc = jnp.dot(h_ref[...], w_ref[...], preferred_element_type=F32)

    def head_norm(g, scale):
        for hd in range(tn // HEAD_DIM):
            sl = slice(hd * HEAD_DIM, (hd + 1) * HEAD_DIM)
            o_ref[:, sl] = _rms(acc[:, sl], g) * scale

    @pl.when(j < q_tiles)
    def _():
        head_norm(qg_ref[...], q_scale)

    @pl.when(jnp.logical_and(j >= q_tiles, j < q_tiles + k_tiles))
    def _():
        head_norm(kg_ref[...], 1.0)

    @pl.when(j >= q_tiles + k_tiles)
    def _():
        o_ref[...] = acc


def _qkv_proj(h, w, q_g, k_g, a_width, tm=1024, tn=512):
    n, d = h.shape
    nout = w.shape[1]
    kern = functools.partial(_qkv_kernel, tn=tn, q_tiles=a_width // tn, k_tiles=a_width // tn,
                             q_scale=HEAD_DIM ** -0.5)
    return pl.pallas_call(
        kern,
        grid=(n // tm, nout // tn),
        in_specs=[pl.BlockSpec((tm, d), lambda i, j: (i, 0)),
                  pl.BlockSpec((d, tn), lambda i, j: (0, j)),
                  pl.BlockSpec((1, HEAD_DIM), lambda i, j: (0, 0)),
                  pl.BlockSpec((1, HEAD_DIM), lambda i, j: (0, 0))],
        out_specs=pl.BlockSpec((tm, tn), lambda i, j: (i, j)),
        out_shape=jax.ShapeDtypeStruct((n, nout), F32),
        compiler_params=_cparams(("parallel", "parallel")),
        name="qkv_proj",
    )(h, w, q_g.reshape(1, HEAD_DIM), k_g.reshape(1, HEAD_DIM))


def _gelu_tanh(x):
    c = math.sqrt(2.0 / math.pi)
    return 0.5 * x * (1.0 + jnp.tanh(c * (x + 0.044715 * (x * x * x))))


def _mm_kernel(h_ref, w_ref, o_ref):
    o_ref[...] = jnp.dot(h_ref[...], w_ref[...], preferred_element_type=F32).astype(o_ref.dtype)


def _mm_bias_gelu_kernel(h_ref, w_ref, b_ref, o_ref):
    acc = jnp.dot(h_ref[...], w_ref[...], preferred_element_type=F32) + b_ref[...]
    o_ref[...] = _gelu_tanh(acc).astype(o_ref.dtype)


def _matmul(h, w, bias=None, tm=1024, tn=512, name="matmul"):
    n, d = h.shape
    m = w.shape[1]
    in_specs = [pl.BlockSpec((tm, d), lambda i, j: (i, 0)),
                pl.BlockSpec((d, tn), lambda i, j: (0, j))]
    args = [h, w]
    kern = _mm_kernel
    if bias is not None:
        in_specs.append(pl.BlockSpec((1, tn), lambda i, j: (0, j)))
        args.append(bias.reshape(1, m))
        kern = _mm_bias_gelu_kernel
    return pl.pallas_call(
        kern,
        grid=(n // tm, m // tn),
        in_specs=in_specs,
        out_specs=pl.BlockSpec((tm, tn), lambda i, j: (i, j)),
        out_shape=jax.ShapeDtypeStruct((n, m), BF16),
        compiler_params=_cparams(("parallel", "parallel")),
        name=name,
    )(*args)


def _t5_bucket_np(dist):
    max_exact = REL_BUCKETS // 2
    d = np.maximum(dist, 0)
    log_ratio = (np.log(np.maximum(d, 1).astype(np.float32) / np.float32(max_exact))
                 / np.float32(math.log(REL_MAX_DIST / max_exact)))
    large = max_exact + (log_ratio.astype(np.float32) * np.float32(REL_BUCKETS - max_exact)).astype(np.int32)
    large = np.minimum(large, REL_BUCKETS - 1)
    return np.where(d < max_exact, d, large).astype(np.int32)


def _branch_bucket_tables():
    blk = ATT_BLK
    qi = np.arange(blk)[:, None]
    kj = np.arange(2 * blk)[None, :]
    dm = qi + blk - kj
    tabs = []
    for _, dil in DILATED_BRANCHES:
        bucket = _t5_bucket_np(dm * dil)
        tabs.append(np.where((dm >= 0) & (dm <= blk), bucket, -1))
    return np.stack(tabs).astype(np.int32)


def _attn_kernel(rb_ref, bidx_ref, q_ref, k_ref, v_ref, o_ref, bias_scr, ob_scr, m_scr, l_scr, *, seq):
    blk = ATT_BLK
    hidx = pl.program_id(0)

    @pl.when(pl.program_id(1) == 0)
    def _():
        for br in range(len(DILATED_BRANCHES)):
            idx = bidx_ref[br]
            bias = jnp.full(idx.shape, NEG, F32)
            for u in range(REL_BUCKETS):
                bias = jnp.where(idx == u, rb_ref[u, hidx], bias)
            bias_scr[br] = bias

    def block(br, q_rows, k_rows, bias):
        qb = q_ref[0, q_rows, :].astype(BF16)
        kb = k_ref[0, k_rows, :].astype(BF16)
        vb = v_ref[0, k_rows, :].astype(BF16)
        s = lax.dot_general(qb, kb, (((1,), (1,)), ((), ())), preferred_element_type=F32) + bias
        m = jnp.max(s, axis=-1, keepdims=True)
        p = jnp.exp(s - m)
        l = jnp.sum(p, axis=-1, keepdims=True)
        o = jnp.dot(p.astype(BF16), vb, preferred_element_type=F32)
        ob_scr[br, q_rows, :] = o
        m_scr[br, q_rows, :] = jnp.broadcast_to(m, (blk, HEAD_DIM))
        l_scr[br, q_rows, :] = jnp.broadcast_to(l, (blk, HEAD_DIM))

    for br, (window, dil) in enumerate(DILATED_BRANCHES):
        n_pos = seq // dil
        nb = -(-n_pos // blk)
        for r in range(dil):
            for nblk in range(nb):
                start = r + dil * blk * nblk

                def rows(first, count):
                    return pl.ds(first, count) if dil == 1 else pl.ds(first, count, stride=dil)

                q_rows = rows(start, blk)
                if nblk == 0:
                    block(br, q_rows, q_rows, bias_scr[br, :, blk:])
                else:
                    block(br, q_rows, rows(start - dil * blk, 2 * blk), bias_scr[br])

    n_br = len(DILATED_BRANCHES)
    m_all = m_scr[0]
    for br in range(1, n_br):
        m_all = jnp.maximum(m_all, m_scr[br])
    num = jnp.zeros((seq, HEAD_DIM), F32)
    den = jnp.zeros((seq, HEAD_DIM), F32)
    for br in range(n_br):
        a = jnp.exp(m_scr[br] - m_all)
        num = num + a * ob_scr[br]
        den = den + a * l_scr[br]
    o_ref[0] = (num / den).astype(o_ref.dtype)


def _attention(qkv, rel_bias, bsz, seq):
    h, e = A_HEADS, HEAD_DIM
    n_br = len(DILATED_BRANCHES)
    bidx = jnp.asarray(_branch_bucket_tables())
    kern = functools.partial(_attn_kernel, seq=seq)
    return pl.pallas_call(
        kern,
        grid=(h, bsz),
        in_specs=[pl.BlockSpec(memory_space=pltpu.SMEM),
                  pl.BlockSpec((n_br, ATT_BLK, 2 * ATT_BLK), lambda hh, b: (0, 0, 0)),
                  pl.BlockSpec((1, seq, e), lambda hh, b: (b, 0, hh)),
                  pl.BlockSpec((1, seq, e), lambda hh, b: (b, 0, h + hh)),
                  pl.BlockSpec((1, seq, e), lambda hh, b: (b, 0, 2 * h + hh))],
        out_specs=pl.BlockSpec((1, seq, e), lambda hh, b: (b, 0, hh)),
        out_shape=jax.ShapeDtypeStruct((bsz, seq, h * e), BF16),
        scratch_shapes=[pltpu.VMEM((n_br, ATT_BLK, 2 * ATT_BLK), F32),
                        pltpu.VMEM((n_br, seq, e), F32),
                        pltpu.VMEM((n_br, seq, e), F32),
                        pltpu.VMEM((n_br, seq, e), F32)],
        compiler_params=_cparams(("arbitrary", "arbitrary")),
        name="dilated_attention",
    )(rel_bias, bidx, qkv, qkv, qkv)


def _conv_kernel(cv_ref, cg_ref, hv_ref, hg_ref, w_ref, b_ref, lg_ref, lb_ref, o_ref, g_scr, y_scr, *, tt):
    halo = CONV_HALO
    t = pl.program_id(1)
    hv = hv_ref[0].astype(F32)
    hg = hg_ref[0].astype(F32)
    g_scr[0:halo, :] = jnp.where(t > 0, hv * jax.nn.sigmoid(hg), 0.0)
    cv = cv_ref[0].astype(F32)
    cg = cg_ref[0].astype(F32)
    g_scr[halo:, :] = cv * jax.nn.sigmoid(cg)

    ch = g_scr.shape[1]
    rc, cc = 32, 256
    first = halo - (CONV_WIDTH - 1)
    for r0 in range(0, tt, rc):
        for c0 in range(0, ch, cc):
            acc = jnp.broadcast_to(b_ref[:, c0:c0 + cc], (rc, cc))
            for k in range(CONV_WIDTH):
                acc = acc + w_ref[k:k + 1, c0:c0 + cc] * g_scr[r0 + first + k:r0 + first + k + rc, c0:c0 + cc]
            y_scr[r0:r0 + rc, c0:c0 + cc] = acc

    y = y_scr[...]
    mu = jnp.mean(y, axis=-1, keepdims=True)
    yc = y - mu
    var = jnp.mean(yc * yc, axis=-1, keepdims=True)
    z = yc * lax.rsqrt(var + EPS) * lg_ref[...] + lb_ref[...]
    o_ref[0] = (z * jax.nn.sigmoid(z)).astype(o_ref.dtype)


def _conv_module(pc, conv_w, conv_b, ln_g, ln_b, tt=256):
    bsz, seq, ch2 = pc.shape
    ch = ch2 // 2
    hb = tt // CONV_HALO
    kern = functools.partial(_conv_kernel, tt=tt)
    vec = lambda a: a.reshape(1, ch)
    return pl.pallas_call(
        kern,
        grid=(bsz, seq // tt),
        in_specs=[pl.BlockSpec((1, tt, ch), lambda b, t: (b, t, 0)),
                  pl.BlockSpec((1, tt, ch), lambda b, t: (b, t, 1)),
                  pl.BlockSpec((1, CONV_HALO, ch), lambda b, t: (b, jnp.maximum(t * hb - 1, 0), 0)),
                  pl.BlockSpec((1, CONV_HALO, ch), lambda b, t: (b, jnp.maximum(t * hb - 1, 0), 1)),
                  pl.BlockSpec((CONV_WIDTH, ch), lambda b, t: (0, 0)),
                  pl.BlockSpec((1, ch), lambda b, t: (0, 0)),
                  pl.BlockSpec((1, ch), lambda b, t: (0, 0)),
                  pl.BlockSpec((1, ch), lambda b, t: (0, 0))],
        out_specs=pl.BlockSpec((1, tt, ch), lambda b, t: (b, t, 0)),
        out_shape=jax.ShapeDtypeStruct((bsz, seq, ch), BF16),
        scratch_shapes=[pltpu.VMEM((tt + CONV_HALO, ch), F32), pltpu.VMEM((tt, ch), F32)],
        compiler_params=_cparams(("parallel", "parallel")),
        name="conv_module",
    )(pc, pc, pc, pc, conv_w, vec(conv_b), vec(ln_g), vec(ln_b))


def _outproj_kernel(a_ref, b_ref, wa_ref, wb_ref, x_ref, g_ref, xo_ref, ho_ref):
    acc = jnp.dot(a_ref[...], wa_ref[...], preferred_element_type=F32)
    acc = acc + jnp.dot(b_ref[...], wb_ref[...], preferred_element_type=F32)
    xn = x_ref[...] + acc
    xo_ref[...] = xn
    ho_ref[...] = _rms(xn, g_ref[...]).astype(ho_ref.dtype)


def _outproj(a, b, wa, wb, x, g, tm=512):
    n, d = x.shape
    ka, kb = a.shape[1], b.shape[1]
    return pl.pallas_call(
        _outproj_kernel,
        grid=(n // tm,),
        in_specs=[pl.BlockSpec((tm, ka), lambda i: (i, 0)),
                  pl.BlockSpec((tm, kb), lambda i: (i, 0)),
                  pl.BlockSpec((ka, d), lambda i: (0, 0)),
                  pl.BlockSpec((kb, d), lambda i: (0, 0)),
                  pl.BlockSpec((tm, d), lambda i: (i, 0)),
                  pl.BlockSpec((1, d), lambda i: (0, 0))],
        out_specs=[pl.BlockSpec((tm, d), lambda i: (i, 0)),
                   pl.BlockSpec((tm, d), lambda i: (i, 0))],
        out_shape=[jax.ShapeDtypeStruct((n, d), F32), jax.ShapeDtypeStruct((n, d), BF16)],
        compiler_params=_cparams(("parallel",)),
        name="out_proj",
    )(a, b, wa, wb, x, g.reshape(1, d))


def _ffn_kernel(h_ref, w1_ref, w3_ref, w2_ref, x_ref, g_ref, xo_ref, ho_ref):
    f = pl.program_id(1)
    h = h_ref[...]
    a = jnp.dot(h, w1_ref[...], preferred_element_type=F32)
    b = jnp.dot(h, w3_ref[...], preferred_element_type=F32)
    hid = (a * jax.nn.sigmoid(a) * b).astype(BF16)
    part = jnp.dot(hid, w2_ref[...], preferred_element_type=F32)

    @pl.when(f == 0)
    def _():
        xo_ref[...] = part

    @pl.when(f > 0)
    def _():
        xo_ref[...] += part

    @pl.when(f == pl.num_programs(1) - 1)
    def _():
        xn = x_ref[...] + xo_ref[...]
        xo_ref[...] = xn
        ho_ref[...] = _rms(xn, g_ref[...]).astype(ho_ref.dtype)


def _ffn(h, w1, w3, w2, x, g, tm=512, tf=512):
    n, d = x.shape
    ff = w1.shape[1]
    return pl.pallas_call(
        _ffn_kernel,
        grid=(n // tm, ff // tf),
        in_specs=[pl.BlockSpec((tm, d), lambda i, f: (i, 0)),
                  pl.BlockSpec((d, tf), lambda i, f: (0, f)),
                  pl.BlockSpec((d, tf), lambda i, f: (0, f)),
                  pl.BlockSpec((tf, d), lambda i, f: (f, 0)),
                  pl.BlockSpec((tm, d), lambda i, f: (i, 0)),
                  pl.BlockSpec((1, d), lambda i, f: (0, 0))],
        out_specs=[pl.BlockSpec((tm, d), lambda i, f: (i, 0)),
                   pl.BlockSpec((tm, d), lambda i, f: (i, 0))],
        out_shape=[jax.ShapeDtypeStruct((n, d), F32), jax.ShapeDtypeStruct((n, d), BF16)],
        compiler_params=_cparams(("parallel", "arbitrary")),
        name="dense_swiglu",
    )(h, w1, w3, w2, x, g.reshape(1, d))


def _gate_kernel(z_ref, ws_ref, bs_ref, vg_ref, vb_ref, wo_ref, x_ref, g_ref, wr_ref,
                 xo_ref, ho_ref, lg_ref, gated_scr, *, tm, width):
    gch = width // GMLP_GROUPS
    row = lax.broadcasted_iota(jnp.int32, (CHUNK, CHUNK), 0)
    col = lax.broadcasted_iota(jnp.int32, (CHUNK, CHUNK), 1)
    causal = row >= col
    for c0 in range(0, tm, CHUNK):
        v = z_ref[c0:c0 + CHUNK, width:].astype(F32)
        mu = jnp.mean(v, axis=-1, keepdims=True)
        vc = v - mu
        var = jnp.mean(vc * vc, axis=-1, keepdims=True)
        vn = (vc * lax.rsqrt(var + EPS) * vg_ref[...] + vb_ref[...]).astype(BF16)
        for gi in range(GMLP_GROUPS):
            cs = slice(gi * gch, (gi + 1) * gch)
            wsg = jnp.where(causal, ws_ref[gi], 0.0).astype(BF16)
            sv = jnp.dot(wsg, vn[:, cs], preferred_element_type=F32) + bs_ref[:, cs]
            u = z_ref[c0:c0 + CHUNK, cs].astype(F32)
            gated_scr[c0:c0 + CHUNK, cs] = (u * sv).astype(BF16)
    xn = x_ref[...] + jnp.dot(gated_scr[...], wo_ref[...], preferred_element_type=F32)
    xo_ref[...] = xn
    hn = _rms(xn, g_ref[...])
    ho_ref[...] = hn.astype(ho_ref.dtype)
    lg_ref[...] = jnp.dot(hn, wr_ref[...], preferred_element_type=F32, precision=lax.Precision.HIGHEST)


def _gate_outproj(z, w_s, b_s, vn_g, vn_b, w_o, x, g, w_router, tm=512):
    n, d = x.shape
    width = z.shape[1] // 2
    gch = width // GMLP_GROUPS
    bs_x = jnp.repeat(b_s.T, gch, axis=1)
    wr = jnp.zeros((d, 128), F32).at[:, :N_EXPERTS].set(w_router)
    kern = functools.partial(_gate_kernel, tm=tm, width=width)
    full = lambda shape: pl.BlockSpec(shape, lambda i: (0,) * len(shape))
    return pl.pallas_call(
        kern,
        grid=(n // tm,),
        in_specs=[pl.BlockSpec((tm, 2 * width), lambda i: (i, 0)),
                  full((GMLP_GROUPS, CHUNK, CHUNK)),
                  full((CHUNK, width)),
                  full((1, width)),
                  full((1, width)),
                  full((width, d)),
                  pl.BlockSpec((tm, d), lambda i: (i, 0)),
                  full((1, d)),
                  full((d, 128))],
        out_specs=[pl.BlockSpec((tm, d), lambda i: (i, 0)),
                   pl.BlockSpec((tm, d), lambda i: (i, 0)),
                   pl.BlockSpec((tm, 128), lambda i: (i, 0))],
        out_shape=[jax.ShapeDtypeStruct((n, d), F32), jax.ShapeDtypeStruct((n, d), F32),
                   jax.ShapeDtypeStruct((n, 128), F32)],
        scratch_shapes=[pltpu.VMEM((tm, width), BF16)],
        compiler_params=_cparams(("parallel",)),
        name="gmlp_gate_outproj",
    )(z, w_s, bs_x, vn_g.reshape(1, width), vn_b.reshape(1, width), w_o, x, g.reshape(1, d), wr)


def _router_kernel(lg_ref, oi_ref, og_ref, cnt_ref, carry_scr, *, tm):
    i = pl.program_id(0)

    @pl.when(i == 0)
    def _():
        carry_scr[...] = jnp.zeros_like(carry_scr)

    lane = lax.broadcasted_iota(jnp.int32, (tm, 128), 1)
    lg = jnp.where(lane < N_EXPERTS, lg_ref[...], -jnp.inf)
    m1 = jnp.max(lg, axis=-1, keepdims=True)
    i1 = jnp.min(jnp.where(lg == m1, lane, 128), axis=-1, keepdims=True)
    lg2 = jnp.where(lane == i1, -jnp.inf, lg)
    m2 = jnp.max(lg2, axis=-1, keepdims=True)
    i2 = jnp.min(jnp.where(lg2 == m2, lane, 128), axis=-1, keepdims=True)
    e2 = jnp.exp(m2 - m1)
    den = 1.0 + e2
    g1 = 1.0 / den
    g2 = e2 / den

    sel = jnp.logical_or(lane == i1, lane == i2)
    row = lax.broadcasted_iota(jnp.int32, (tm, tm), 0)
    col = lax.broadcasted_iota(jnp.int32, (tm, tm), 1)
    before = jnp.where(row > col, 1.0, 0.0).astype(BF16)
    selb = jnp.where(sel, 1.0, 0.0)
    tot = jnp.dot(before, selb.astype(BF16), preferred_element_type=F32) + carry_scr[0:1, :]
    r1 = jnp.sum(jnp.where(lane == i1, tot, 0.0), axis=-1, keepdims=True).astype(jnp.int32)
    r2 = jnp.sum(jnp.where(lane == i2, tot, 0.0), axis=-1, keepdims=True).astype(jnp.int32)
    new_carry = carry_scr[0:1, :] + jnp.sum(selb, axis=0, keepdims=True)
    carry_scr[...] = jnp.broadcast_to(new_carry, carry_scr.shape)
    cnt_ref[...] = jnp.broadcast_to(new_carry, cnt_ref.shape)

    oi_ref[...] = jnp.where(lane == 0, i1, jnp.where(lane == 1, i2, jnp.where(lane == 2, r1, r2)))
    og_ref[...] = jnp.where(lane == 0, g1, g2)


def _router(logits, tm=512):
    n = logits.shape[0]
    kern = functools.partial(_router_kernel, tm=tm)
    return pl.pallas_call(
        kern,
        grid=(n // tm,),
        in_specs=[pl.BlockSpec((tm, 128), lambda i: (i, 0))],
        out_specs=[pl.BlockSpec((tm, 128), lambda i: (i, 0)),
                   pl.BlockSpec((tm, 128), lambda i: (i, 0)),
                   pl.BlockSpec((8, 128), lambda i: (0, 0))],
        out_shape=[jax.ShapeDtypeStruct((n, 128), jnp.int32), jax.ShapeDtypeStruct((n, 128), F32),
                   jax.ShapeDtypeStruct((8, 128), F32)],
        scratch_shapes=[pltpu.VMEM((8, 128), F32)],
        compiler_params=_cparams(("arbitrary",)),
        name="router_top2",
    )(logits)


def _gather_rows(src_ref, idx_ref, base, buf, sem, tt):
    def copy(r):
        return pltpu.make_async_copy(src_ref.at[pl.ds(idx_ref[base + r], 1)], buf.at[pl.ds(r, 1)], sem)

    def start(r, carry):
        copy(r).start()
        return carry

    def wait(r, carry):
        copy(r).wait()
        return carry

    lax.fori_loop(0, tt, start, 0, unroll=8)
    lax.fori_loop(0, tt, wait, 0, unroll=8)


def _dispatch_kernel(tok_ref, nt_ref, h_ref, xs_ref, buf, sem, *, tt):
    i = pl.program_id(0)

    @pl.when(i < nt_ref[0])
    def _():
        _gather_rows(h_ref, tok_ref, i * tt, buf, sem, tt)
        xs_ref[...] = buf[...].astype(xs_ref.dtype)

    @pl.when(i >= nt_ref[0])
    def _():
        xs_ref[...] = jnp.zeros_like(xs_ref)


def _dispatch(h, row_token, n_live, tt):
    n, d = h.shape
    n_rows = row_token.shape[0]
    kern = functools.partial(_dispatch_kernel, tt=tt)
    return pl.pallas_call(
        kern,
        grid_spec=pltpu.PrefetchScalarGridSpec(
            num_scalar_prefetch=2,
            grid=(n_rows // tt,),
            in_specs=[pl.BlockSpec(memory_space=pl.ANY)],
            out_specs=pl.BlockSpec((tt, d), lambda i, tok, nt: (i, 0)),
            scratch_shapes=[pltpu.VMEM((tt, d), F32), pltpu.SemaphoreType.DMA(())]),
        out_shape=jax.ShapeDtypeStruct((n_rows, d), BF16),
        compiler_params=_cparams(("arbitrary",)),
        name="moe_dispatch",
    )(row_token, n_live, h)


def _moe_kernel(te_ref, nt_ref, xs_ref, wg_ref, wu_ref, wd_ref, ys_ref):
    i = pl.program_id(0)
    f = pl.program_id(1)

    @pl.when(i < nt_ref[0])
    def _():
        xs = xs_ref[...]
        a = jnp.dot(xs, wg_ref[0].astype(BF16), preferred_element_type=F32)
        b = jnp.dot(xs, wu_ref[0].astype(BF16), preferred_element_type=F32)
        hid = (a * jax.nn.sigmoid(a) * b).astype(BF16)
        part = jnp.dot(hid, wd_ref[0].astype(BF16), preferred_element_type=F32)

        @pl.when(f == 0)
        def _():
            ys_ref[...] = part

        @pl.when(f > 0)
        def _():
            ys_ref[...] += part

    @pl.when(jnp.logical_and(i >= nt_ref[0], f == 0))
    def _():
        ys_ref[...] = jnp.zeros_like(ys_ref)


def _moe_experts(xs, tile_expert, n_live, w_gate, w_up, w_down, tm=MOE_TM, tf=MOE_TF):
    n_rows, d = xs.shape
    ff = w_gate.shape[2]
    n_tiles = n_rows // tm
    n_f = ff // tf

    def live_tile(i, nt):
        return jnp.minimum(i, nt[0] - 1)

    def live_f(i, f, nt):
        return jnp.where(i < nt[0], f, n_f - 1)

    return pl.pallas_call(
        _moe_kernel,
        grid_spec=pltpu.PrefetchScalarGridSpec(
            num_scalar_prefetch=2,
            grid=(n_tiles, n_f),
            in_specs=[pl.BlockSpec((tm, d), lambda i, f, te, nt: (live_tile(i, nt), 0),
                                   pipeline_mode=pl.Buffered(1)),
                      pl.BlockSpec((1, d, tf), lambda i, f, te, nt: (te[i], 0, live_f(i, f, nt))),
                      pl.BlockSpec((1, d, tf), lambda i, f, te, nt: (te[i], 0, live_f(i, f, nt))),
                      pl.BlockSpec((1, tf, d), lambda i, f, te, nt: (te[i], live_f(i, f, nt), 0))],
            out_specs=pl.BlockSpec((tm, d), lambda i, f, te, nt: (i, 0), pipeline_mode=pl.Buffered(1))),
        out_shape=jax.ShapeDtypeStruct((n_rows, d), F32),
        compiler_params=_cparams(("arbitrary", "arbitrary")),
        name="moe_experts",
    )(tile_expert, n_live, xs, w_gate, w_up, w_down)


def _combine_kernel(p1_ref, p2_ref, ys_ref, x_ref, g_ref, o_ref, buf1, buf2, sem, *, tt):
    base = pl.program_id(0) * tt

    def copies(t):
        return (pltpu.make_async_copy(ys_ref.at[pl.ds(p1_ref[base + t], 1)], buf1.at[pl.ds(t, 1)], sem),
                pltpu.make_async_copy(ys_ref.at[pl.ds(p2_ref[base + t], 1)], buf2.at[pl.ds(t, 1)], sem))

    def start(t, carry):
        for c in copies(t):
            c.start()
        return carry

    def wait(t, carry):
        for c in copies(t):
            c.wait()
        return carry

    lax.fori_loop(0, tt, start, 0, unroll=8)
    lax.fori_loop(0, tt, wait, 0, unroll=8)
    g = g_ref[...]
    o_ref[...] = x_ref[...] + (buf1[...] * g[:, 0:1] + buf2[...] * g[:, 1:2])


def _combine(ys, pos1, pos2, gates, x, tt=256):
    n, d = x.shape
    kern = functools.partial(_combine_kernel, tt=tt)
    return pl.pallas_call(
        kern,
        grid_spec=pltpu.PrefetchScalarGridSpec(
            num_scalar_prefetch=2,
            grid=(n // tt,),
            in_specs=[pl.BlockSpec(memory_space=pl.ANY),
                      pl.BlockSpec((tt, d), lambda i, p1, p2: (i, 0)),
                      pl.BlockSpec((tt, 128), lambda i, p1, p2: (i, 0))],
            out_specs=pl.BlockSpec((tt, d), lambda i, p1, p2: (i, 0)),
            scratch_shapes=[pltpu.VMEM((tt, d), F32), pltpu.VMEM((tt, d), F32),
                            pltpu.SemaphoreType.DMA(())]),
        out_shape=jax.ShapeDtypeStruct((n, d), F32),
        compiler_params=_cparams(("arbitrary",)),
        name="moe_combine",
    )(pos1, pos2, ys, x, gates)


def _even_layer(x, rel_bias, norm_mix, w_in, q_g, k_g, conv_w, conv_b, cn_g, cn_b, w_out, norm_next,
                bsz, seq):
    n, d = x.shape
    a_width = A_HEADS * HEAD_DIM
    h = _rmsnorm(x, norm_mix)
    w_in = w_in.astype(BF16)
    qkv = _qkv_proj(h, w_in[:, :3 * a_width], q_g, k_g, a_width)
    pc = _matmul(h, w_in[:, 3 * a_width:], name="conv_proj")
    ch = pc.shape[1] // 2
    o_a = _attention(qkv.reshape(bsz, seq, 3 * a_width), rel_bias, bsz, seq).reshape(n, a_width)
    o_b = _conv_module(pc.reshape(bsz, seq, 2 * ch), conv_w, conv_b, cn_g, cn_b).reshape(n, ch)
    w_out = w_out.astype(BF16)
    return _outproj(o_a, o_b, w_out[:a_width], w_out[a_width:], x, norm_next)


def _moe_layer(x, h, logits, w_gate, w_up, w_down):
    n, d = x.shape
    tm = MOE_TM
    oi, og, cnt = _router(logits)
    e1, e2, r1, r2 = oi[:, 0], oi[:, 1], oi[:, 2], oi[:, 3]
    counts = cnt[0, :N_EXPERTS].astype(jnp.int32)
    padded = ((counts + tm - 1) // tm) * tm
    ends = jnp.cumsum(padded)
    starts = ends - padded
    pos1 = starts[e1] + r1
    pos2 = starts[e2] + r2
    n_tiles = (2 * n) // tm + N_EXPERTS
    n_live = (ends[-1] // tm).astype(jnp.int32).reshape(1)
    tile_row = jnp.minimum(jnp.arange(n_tiles, dtype=jnp.int32), n_live[0] - 1) * tm
    tile_expert = jnp.sum(tile_row[:, None] >= ends[None, :], axis=1).astype(jnp.int32)
    tok = jnp.arange(n, dtype=jnp.int32)
    row_token = jnp.zeros((n_tiles * tm,), jnp.int32).at[pos1].set(tok).at[pos2].set(tok)
    xs = _dispatch(h, row_token, n_live, tm)
    ys = _moe_experts(xs, tile_expert, n_live, w_gate, w_up, w_down)
    return _combine(ys, pos1, pos2, og, x)


def kernel(x, rel_bias, even_norm_mix, even_w_in, even_q_norm, even_k_norm, even_conv_w, even_conv_b, even_cnorm_g, even_cnorm_b, even_w_out, even_norm_ffn, even_ffn_w1, even_ffn_w3, even_ffn_w2, odd_norm_mix, odd_w_u, odd_b_u, odd_vnorm_g, odd_vnorm_b, odd_w_s, odd_b_s, odd_w_o, odd_norm_ffn, odd_router, odd_we_gate, odd_we_up, odd_we_down):
    bsz, seq, d = x.shape
    xf = x.reshape(bsz * seq, d)
    xf, h = _even_layer(xf, rel_bias, even_norm_mix[0], even_w_in[0], even_q_norm[0], even_k_norm[0],
                        even_conv_w[0], even_conv_b[0], even_cnorm_g[0], even_cnorm_b[0], even_w_out[0],
                        even_norm_ffn[0], bsz, seq)
    xf, h = _ffn(h, even_ffn_w1[0].astype(BF16), even_ffn_w3[0].astype(BF16), even_ffn_w2[0].astype(BF16),
                 xf, odd_norm_mix[0])
    z = _matmul(h, odd_w_u[0].astype(BF16), bias=odd_b_u[0], name="gmlp_in_proj")
    xf, h, logits = _gate_outproj(z, odd_w_s[0], odd_b_s[0], odd_vnorm_g[0], odd_vnorm_b[0],
                                  odd_w_o[0].astype(BF16), xf, odd_norm_ffn[0], odd_router[0])
    xf = _moe_layer(xf, h, logits, odd_we_gate[0], odd_we_up[0], odd_we_down[0])
    return xf.reshape(bsz, seq, d)
```

```python
import functools
import math

import numpy as np
import jax
import jax.numpy as jnp
from jax import lax
from jax.experimental import pallas as pl
from jax.experimental.pallas import tpu as pltpu

F32 = jnp.float32
BF16 = jnp.bfloat16
EPS = 1e-6

HEAD_DIM = 128
A_HEADS = 8
DILATED_BRANCHES = ((128, 1), (512, 4), (2048, 16))
ATT_BLK = 128
REL_BUCKETS = 32
REL_MAX_DIST = 2048
CONV_WIDTH = 31
CONV_HALO = 32
CHUNK = 128
GMLP_GROUPS = 16
N_EXPERTS = 8
NEG = -1e30

VMEM_LIMIT_BYTES = 56 * 1024 * 1024
MOE_TM = 1024
MOE_TF = 512
SWIGLU_UP_CHUNK = 256
SWIGLU_DOWN_CHUNK = 512


def _cparams(sem):
    return pltpu.CompilerParams(dimension_semantics=sem, vmem_limit_bytes=VMEM_LIMIT_BYTES)


def _rms(x, g):
    ms = jnp.mean(x * x, axis=-1, keepdims=True)
    return x * lax.rsqrt(ms + EPS) * g


def _rmsnorm_kernel(x_ref, g_ref, o_ref):
    o_ref[...] = _rms(x_ref[...], g_ref[...]).astype(o_ref.dtype)


def _rmsnorm(x, g, tm=512):
    n, d = x.shape
    return pl.pallas_call(
        _rmsnorm_kernel,
        grid=(n // tm,),
        in_specs=[pl.BlockSpec((tm, d), lambda i: (i, 0)),
                  pl.BlockSpec((1, d), lambda i: (0, 0))],
        out_specs=pl.BlockSpec((tm, d), lambda i: (i, 0)),
        out_shape=jax.ShapeDtypeStruct((n, d), BF16),
        compiler_params=_cparams(("parallel",)),
        name="rmsnorm",
    )(x, g.reshape(1, d))


def _qkv_kernel(h_ref, w_ref, qg_ref, kg_ref, o_ref, *, tn, q_tiles, k_tiles, q_scale):
    j = pl.program_id(1)
    acc = jnp.dot(h_ref[...], w_ref[...], preferred_element_type=F32)

    def head_norm(g, scale):
        for hd in range(tn // HEAD_DIM):
            sl = slice(hd * HEAD_DIM, (hd + 1) * HEAD_DIM)
            o_ref[:, sl] = _rms(acc[:, sl], g) * scale

    @pl.when(j < q_tiles)
    def _():
        head_norm(qg_ref[...], q_scale)

    @pl.when(jnp.logical_and(j >= q_tiles, j < q_tiles + k_tiles))
    def _():
        head_norm(kg_ref[...], 1.0)

    @pl.when(j >= q_tiles + k_tiles)
    def _():
        o_ref[...] = acc


def _qkv_proj(h, w, q_g, k_g, a_width, tm=1024, tn=1024):
    n, d = h.shape
    nout = w.shape[1]
    kern = functools.partial(_qkv_kernel, tn=tn, q_tiles=a_width // tn, k_tiles=a_width // tn,
                             q_scale=HEAD_DIM ** -0.5)
    return pl.pallas_call(
        kern,
        grid=(n // tm, nout // tn),
        in_specs=[pl.BlockSpec((tm, d), lambda i, j: (i, 0)),
                  pl.BlockSpec((d, tn), lambda i, j: (0, j)),
                  pl.BlockSpec((1, HEAD_DIM), lambda i, j: (0, 0)),
                  pl.BlockSpec((1, HEAD_DIM), lambda i, j: (0, 0))],
        out_specs=pl.BlockSpec((tm, tn), lambda i, j: (i, j)),
        out_shape=jax.ShapeDtypeStruct((n, nout), F32),
        compiler_params=_cparams(("parallel", "parallel")),
        name="qkv_proj",
    )(h, w, q_g.reshape(1, HEAD_DIM), k_g.reshape(1, HEAD_DIM))


def _gelu_tanh(x):
    c = math.sqrt(2.0 / math.pi)
    return 0.5 * x * (1.0 + jnp.tanh(c * (x + 0.044715 * (x * x * x))))


def _mm_kernel(h_ref, w_ref, o_ref):
    o_ref[...] = jnp.dot(h_ref[...], w_ref[...], preferred_element_type=F32).astype(o_ref.dtype)


def _mm_bias_gelu_kernel(h_ref, w_ref, b_ref, o_ref):
    acc = jnp.dot(h_ref[...], w_ref[...], preferred_element_type=F32) + b_ref[...]
    o_ref[...] = _gelu_tanh(acc).astype(o_ref.dtype)


def _matmul(h, w, bias=None, tm=1024, tn=1024, name="matmul"):
    n, d = h.shape
    m = w.shape[1]
    in_specs = [pl.BlockSpec((tm, d), lambda i, j: (i, 0)),
                pl.BlockSpec((d, tn), lambda i, j: (0, j))]
    args = [h, w]
    kern = _mm_kernel
    if bias is not None:
        in_specs.append(pl.BlockSpec((1, tn), lambda i, j: (0, j)))
        args.append(bias.reshape(1, m))
        kern = _mm_bias_gelu_kernel
    return pl.pallas_call(
        kern,
        grid=(n // tm, m // tn),
        in_specs=in_specs,
        out_specs=pl.BlockSpec((tm, tn), lambda i, j: (i, j)),
        out_shape=jax.ShapeDtypeStruct((n, m), BF16),
        compiler_params=_cparams(("parallel", "parallel")),
        name=name,
    )(*args)


def _t5_bucket_np(dist):
    max_exact = REL_BUCKETS // 2
    d = np.maximum(dist, 0)
    log_ratio = (np.log(np.maximum(d, 1).astype(np.float32) / np.float32(max_exact))
                 / np.float32(math.log(REL_MAX_DIST / max_exact)))
    large = max_exact + (log_ratio.astype(np.float32) * np.float32(REL_BUCKETS - max_exact)).astype(np.int32)
    large = np.minimum(large, REL_BUCKETS - 1)
    return np.where(d < max_exact, d, large).astype(np.int32)


def _branch_bucket_tables():
    blk = ATT_BLK
    qi = np.arange(blk)[:, None]
    kj = np.arange(2 * blk)[None, :]
    dm = qi + blk - kj
    tabs = []
    for _, dil in DILATED_BRANCHES:
        bucket = _t5_bucket_np(dm * dil)
        tabs.append(np.where((dm >= 0) & (dm <= blk), bucket, -1))
    return np.stack(tabs).astype(np.int32)


def _attn_kernel(rb_ref, bidx_ref, q_ref, k_ref, v_ref, o_ref, bias_scr, ob_scr, m_scr, l_scr, *, seq):
    blk = ATT_BLK
    hidx = pl.program_id(0)

    @pl.when(pl.program_id(1) == 0)
    def _():
        for br in range(len(DILATED_BRANCHES)):
            idx = bidx_ref[br]
            bias = jnp.full(idx.shape, NEG, F32)
            for u in range(REL_BUCKETS):
                bias = jnp.where(idx == u, rb_ref[u, hidx], bias)
            bias_scr[br] = bias

    def block(br, q_rows, k_rows, bias):
        qb = q_ref[0, q_rows, :].astype(BF16)
        kb = k_ref[0, k_rows, :].astype(BF16)
        vb = v_ref[0, k_rows, :].astype(BF16)
        s = lax.dot_general(qb, kb, (((1,), (1,)), ((), ())), preferred_element_type=F32) + bias
        m = jnp.max(s, axis=-1, keepdims=True)
        p = jnp.exp(s - m)
        l = jnp.sum(p, axis=-1, keepdims=True)
        o = jnp.dot(p.astype(BF16), vb, preferred_element_type=F32)
        ob_scr[br, q_rows, :] = o
        m_scr[br, q_rows, :] = jnp.broadcast_to(m, (blk, HEAD_DIM))
        l_scr[br, q_rows, :] = jnp.broadcast_to(l, (blk, HEAD_DIM))

    for br, (window, dil) in enumerate(DILATED_BRANCHES):
        n_pos = seq // dil
        nb = -(-n_pos // blk)
        for r in range(dil):
            for nblk in range(nb):
                start = r + dil * blk * nblk

                def rows(first, count):
                    return pl.ds(first, count) if dil == 1 else pl.ds(first, count, stride=dil)

                q_rows = rows(start, blk)
                if nblk == 0:
                    block(br, q_rows, q_rows, bias_scr[br, :, blk:])
                else:
                    block(br, q_rows, rows(start - dil * blk, 2 * blk), bias_scr[br])

    n_br = len(DILATED_BRANCHES)
    m_all = m_scr[0]
    for br in range(1, n_br):
        m_all = jnp.maximum(m_all, m_scr[br])
    num = jnp.zeros((seq, HEAD_DIM), F32)
    den = jnp.zeros((seq, HEAD_DIM), F32)
    for br in range(n_br):
        a = jnp.exp(m_scr[br] - m_all)
        num = num + a * ob_scr[br]
        den = den + a * l_scr[br]
    o_ref[0] = (num / den).astype(o_ref.dtype)


def _attention(qkv, rel_bias, bsz, seq):
    h, e = A_HEADS, HEAD_DIM
    n_br = len(DILATED_BRANCHES)
    bidx = jnp.asarray(_branch_bucket_tables())
    kern = functools.partial(_attn_kernel, seq=seq)
    return pl.pallas_call(
        kern,
        grid=(h, bsz),
        in_specs=[pl.BlockSpec(memory_space=pltpu.SMEM),
                  pl.BlockSpec((n_br, ATT_BLK, 2 * ATT_BLK), lambda hh, b: (0, 0, 0)),
                  pl.BlockSpec((1, seq, e), lambda hh, b: (b, 0, hh)),
                  pl.BlockSpec((1, seq, e), lambda hh, b: (b, 0, h + hh)),
                  pl.BlockSpec((1, seq, e), lambda hh, b: (b, 0, 2 * h + hh))],
        out_specs=pl.BlockSpec((1, seq, e), lambda hh, b: (b, 0, hh)),
        out_shape=jax.ShapeDtypeStruct((bsz, seq, h * e), BF16),
        scratch_shapes=[pltpu.VMEM((n_br, ATT_BLK, 2 * ATT_BLK), F32),
                        pltpu.VMEM((n_br, seq, e), F32),
                        pltpu.VMEM((n_br, seq, e), F32),
                        pltpu.VMEM((n_br, seq, e), F32)],
        compiler_params=_cparams(("arbitrary", "arbitrary")),
        name="dilated_attention",
    )(rel_bias, bidx, qkv, qkv, qkv)


def _conv_kernel(cv_ref, cg_ref, hv_ref, hg_ref, w_ref, b_ref, lg_ref, lb_ref, o_ref, g_scr, y_scr, *, tt):
    halo = CONV_HALO
    t = pl.program_id(1)
    hv = hv_ref[0].astype(F32)
    hg = hg_ref[0].astype(F32)
    g_scr[0:halo, :] = jnp.where(t > 0, hv * jax.nn.sigmoid(hg), 0.0)
    cv = cv_ref[0].astype(F32)
    cg = cg_ref[0].astype(F32)
    g_scr[halo:, :] = cv * jax.nn.sigmoid(cg)

    ch = g_scr.shape[1]
    rc, cc = 32, 256
    first = halo - (CONV_WIDTH - 1)
    for r0 in range(0, tt, rc):
        for c0 in range(0, ch, cc):
            acc = jnp.broadcast_to(b_ref[:, c0:c0 + cc], (rc, cc))
            for k in range(CONV_WIDTH):
                acc = acc + w_ref[k:k + 1, c0:c0 + cc] * g_scr[r0 + first + k:r0 + first + k + rc, c0:c0 + cc]
            y_scr[r0:r0 + rc, c0:c0 + cc] = acc

    y = y_scr[...]
    mu = jnp.mean(y, axis=-1, keepdims=True)
    yc = y - mu
    var = jnp.mean(yc * yc, axis=-1, keepdims=True)
    z = yc * lax.rsqrt(var + EPS) * lg_ref[...] + lb_ref[...]
    o_ref[0] = (z * jax.nn.sigmoid(z)).astype(o_ref.dtype)


def _conv_module(pc, conv_w, conv_b, ln_g, ln_b, tt=256):
    bsz, seq, ch2 = pc.shape
    ch = ch2 // 2
    hb = tt // CONV_HALO
    kern = functools.partial(_conv_kernel, tt=tt)
    vec = lambda a: a.reshape(1, ch)
    return pl.pallas_call(
        kern,
        grid=(bsz, seq // tt),
        in_specs=[pl.BlockSpec((1, tt, ch), lambda b, t: (b, t, 0)),
                  pl.BlockSpec((1, tt, ch), lambda b, t: (b, t, 1)),
                  pl.BlockSpec((1, CONV_HALO, ch), lambda b, t: (b, jnp.maximum(t * hb - 1, 0), 0)),
                  pl.BlockSpec((1, CONV_HALO, ch), lambda b, t: (b, jnp.maximum(t * hb - 1, 0), 1)),
                  pl.BlockSpec((CONV_WIDTH, ch), lambda b, t: (0, 0)),
                  pl.BlockSpec((1, ch), lambda b, t: (0, 0)),
                  pl.BlockSpec((1, ch), lambda b, t: (0, 0)),
                  pl.BlockSpec((1, ch), lambda b, t: (0, 0))],
        out_specs=pl.BlockSpec((1, tt, ch), lambda b, t: (b, t, 0)),
        out_shape=jax.ShapeDtypeStruct((bsz, seq, ch), BF16),
        scratch_shapes=[pltpu.VMEM((tt + CONV_HALO, ch), F32), pltpu.VMEM((tt, ch), F32)],
        compiler_params=_cparams(("parallel", "parallel")),
        name="conv_module",
    )(pc, pc, pc, pc, conv_w, vec(conv_b), vec(ln_g), vec(ln_b))


def _outproj_kernel(a_ref, b_ref, wa_ref, wb_ref, x_ref, g_ref, xo_ref, ho_ref):
    acc = jnp.dot(a_ref[...], wa_ref[...], preferred_element_type=F32)
    acc = acc + jnp.dot(b_ref[...], wb_ref[...], preferred_element_type=F32)
    xn = x_ref[...] + acc
    xo_ref[...] = xn
    ho_ref[...] = _rms(xn, g_ref[...]).astype(ho_ref.dtype)


def _outproj(a, b, wa, wb, x, g, tm=512):
    n, d = x.shape
    ka, kb = a.shape[1], b.shape[1]
    return pl.pallas_call(
        _outproj_kernel,
        grid=(n // tm,),
        in_specs=[pl.BlockSpec((tm, ka), lambda i: (i, 0)),
                  pl.BlockSpec((tm, kb), lambda i: (i, 0)),
                  pl.BlockSpec((ka, d), lambda i: (0, 0)),
                  pl.BlockSpec((kb, d), lambda i: (0, 0)),
                  pl.BlockSpec((tm, d), lambda i: (i, 0)),
                  pl.BlockSpec((1, d), lambda i: (0, 0))],
        out_specs=[pl.BlockSpec((tm, d), lambda i: (i, 0)),
                   pl.BlockSpec((tm, d), lambda i: (i, 0))],
        out_shape=[jax.ShapeDtypeStruct((n, d), F32), jax.ShapeDtypeStruct((n, d), BF16)],
        compiler_params=_cparams(("parallel",)),
        name="out_proj",
    )(a, b, wa, wb, x, g.reshape(1, d))


def _swiglu_step(x, load_gate, load_up, load_down, tf, acc_ref):
    hid = []
    for c0 in range(0, tf, SWIGLU_UP_CHUNK):
        cols = slice(c0, c0 + SWIGLU_UP_CHUNK)
        a = jnp.dot(x, load_gate(cols), preferred_element_type=F32)
        b = jnp.dot(x, load_up(cols), preferred_element_type=F32)
        hid.append((a * jax.nn.sigmoid(a) * b).astype(BF16))
    hid = jnp.concatenate(hid, axis=1)
    for n0 in range(0, acc_ref.shape[1], SWIGLU_DOWN_CHUNK):
        cols = slice(n0, n0 + SWIGLU_DOWN_CHUNK)
        acc_ref[:, cols] += jnp.dot(hid, load_down(cols), preferred_element_type=F32)


def _ffn_kernel(h_ref, w1_ref, w3_ref, w2_ref, x_ref, g_ref, xo_ref, ho_ref, *, tf):
    f = pl.program_id(1)

    @pl.when(f == 0)
    def _():
        xo_ref[...] = jnp.zeros_like(xo_ref)

    _swiglu_step(h_ref[...], lambda c: w1_ref[:, c], lambda c: w3_ref[:, c], lambda c: w2_ref[:, c],
                 tf, xo_ref)

    @pl.when(f == pl.num_programs(1) - 1)
    def _():
        xn = x_ref[...] + xo_ref[...]
        xo_ref[...] = xn
        ho_ref[...] = _rms(xn, g_ref[...]).astype(ho_ref.dtype)


def _ffn(h, w1, w3, w2, x, g, tm=512, tf=512):
    n, d = x.shape
    ff = w1.shape[1]
    return pl.pallas_call(
        functools.partial(_ffn_kernel, tf=tf),
        grid=(n // tm, ff // tf),
        in_specs=[pl.BlockSpec((tm, d), lambda i, f: (i, 0)),
                  pl.BlockSpec((d, tf), lambda i, f: (0, f)),
                  pl.BlockSpec((d, tf), lambda i, f: (0, f)),
                  pl.BlockSpec((tf, d), lambda i, f: (f, 0)),
                  pl.BlockSpec((tm, d), lambda i, f: (i, 0)),
                  pl.BlockSpec((1, d), lambda i, f: (0, 0))],
        out_specs=[pl.BlockSpec((tm, d), lambda i, f: (i, 0)),
                   pl.BlockSpec((tm, d), lambda i, f: (i, 0))],
        out_shape=[jax.ShapeDtypeStruct((n, d), F32), jax.ShapeDtypeStruct((n, d), BF16)],
        compiler_params=_cparams(("parallel", "arbitrary")),
        name="dense_swiglu",
    )(h, w1, w3, w2, x, g.reshape(1, d))


def _gate_kernel(z_ref, ws_ref, bs_ref, vg_ref, vb_ref, wo_ref, x_ref, g_ref, wr_ref,
                 xo_ref, ho_ref, lg_ref, gated_scr, *, tm, width):
    gch = width // GMLP_GROUPS
    row = lax.broadcasted_iota(jnp.int32, (CHUNK, CHUNK), 0)
    col = lax.broadcasted_iota(jnp.int32, (CHUNK, CHUNK), 1)
    causal = row >= col
    for c0 in range(0, tm, CHUNK):
        v = z_ref[c0:c0 + CHUNK, width:].astype(F32)
        mu = jnp.mean(v, axis=-1, keepdims=True)
        vc = v - mu
        var = jnp.mean(vc * vc, axis=-1, keepdims=True)
        vn = (vc * lax.rsqrt(var + EPS) * vg_ref[...] + vb_ref[...]).astype(BF16)
        for gi in range(GMLP_GROUPS):
            cs = slice(gi * gch, (gi + 1) * gch)
            wsg = jnp.where(causal, ws_ref[gi], 0.0).astype(BF16)
            sv = jnp.dot(wsg, vn[:, cs], preferred_element_type=F32) + bs_ref[:, cs]
            u = z_ref[c0:c0 + CHUNK, cs].astype(F32)
            gated_scr[c0:c0 + CHUNK, cs] = (u * sv).astype(BF16)
    xn = x_ref[...] + jnp.dot(gated_scr[...], wo_ref[...], preferred_element_type=F32)
    xo_ref[...] = xn
    hn = _rms(xn, g_ref[...])
    ho_ref[...] = hn.astype(ho_ref.dtype)
    lg_ref[...] = jnp.dot(hn, wr_ref[...], preferred_element_type=F32, precision=lax.Precision.HIGHEST)


def _gate_outproj(z, w_s, b_s, vn_g, vn_b, w_o, x, g, w_router, tm=512):
    n, d = x.shape
    width = z.shape[1] // 2
    gch = width // GMLP_GROUPS
    bs_x = jnp.repeat(b_s.T, gch, axis=1)
    wr = jnp.zeros((d, 128), F32).at[:, :N_EXPERTS].set(w_router)
    kern = functools.partial(_gate_kernel, tm=tm, width=width)
    full = lambda shape: pl.BlockSpec(shape, lambda i: (0,) * len(shape))
    return pl.pallas_call(
        kern,
        grid=(n // tm,),
        in_specs=[pl.BlockSpec((tm, 2 * width), lambda i: (i, 0)),
                  full((GMLP_GROUPS, CHUNK, CHUNK)),
                  full((CHUNK, width)),
                  full((1, width)),
                  full((1, width)),
                  full((width, d)),
                  pl.BlockSpec((tm, d), lambda i: (i, 0)),
                  full((1, d)),
                  full((d, 128))],
        out_specs=[pl.BlockSpec((tm, d), lambda i: (i, 0)),
                   pl.BlockSpec((tm, d), lambda i: (i, 0)),
                   pl.BlockSpec((tm, 128), lambda i: (i, 0))],
        out_shape=[jax.ShapeDtypeStruct((n, d), F32), jax.ShapeDtypeStruct((n, d), F32),
                   jax.ShapeDtypeStruct((n, 128), F32)],
        scratch_shapes=[pltpu.VMEM((tm, width), BF16)],
        compiler_params=_cparams(("parallel",)),
        name="gmlp_gate_outproj",
    )(z, w_s, bs_x, vn_g.reshape(1, width), vn_b.reshape(1, width), w_o, x, g.reshape(1, d), wr)


def _router_kernel(lg_ref, oi_ref, og_ref, cnt_ref, carry_scr, *, tm):
    i = pl.program_id(0)

    @pl.when(i == 0)
    def _():
        carry_scr[...] = jnp.zeros_like(carry_scr)

    lane = lax.broadcasted_iota(jnp.int32, (tm, 128), 1)
    lg = jnp.where(lane < N_EXPERTS, lg_ref[...], -jnp.inf)
    m1 = jnp.max(lg, axis=-1, keepdims=True)
    i1 = jnp.min(jnp.where(lg == m1, lane, 128), axis=-1, keepdims=True)
    lg2 = jnp.where(lane == i1, -jnp.inf, lg)
    m2 = jnp.max(lg2, axis=-1, keepdims=True)
    i2 = jnp.min(jnp.where(lg2 == m2, lane, 128), axis=-1, keepdims=True)
    e2 = jnp.exp(m2 - m1)
    den = 1.0 + e2
    g1 = 1.0 / den
    g2 = e2 / den

    sel = jnp.logical_or(lane == i1, lane == i2)
    row = lax.broadcasted_iota(jnp.int32, (tm, tm), 0)
    col = lax.broadcasted_iota(jnp.int32, (tm, tm), 1)
    before = jnp.where(row > col, 1.0, 0.0).astype(BF16)
    selb = jnp.where(sel, 1.0, 0.0)
    tot = jnp.dot(before, selb.astype(BF16), preferred_element_type=F32) + carry_scr[0:1, :]
    r1 = jnp.sum(jnp.where(lane == i1, tot, 0.0), axis=-1, keepdims=True).astype(jnp.int32)
    r2 = jnp.sum(jnp.where(lane == i2, tot, 0.0), axis=-1, keepdims=True).astype(jnp.int32)
    new_carry = carry_scr[0:1, :] + jnp.sum(selb, axis=0, keepdims=True)
    carry_scr[...] = jnp.broadcast_to(new_carry, carry_scr.shape)
    cnt_ref[...] = jnp.broadcast_to(new_carry, cnt_ref.shape)

    oi_ref[...] = jnp.where(lane == 0, i1, jnp.where(lane == 1, i2, jnp.where(lane == 2, r1, r2)))
    og_ref[...] = jnp.where(lane == 0, g1, g2)


def _router(logits, tm=512):
    n = logits.shape[0]
    kern = functools.partial(_router_kernel, tm=tm)
    return pl.pallas_call(
        kern,
        grid=(n // tm,),
        in_specs=[pl.BlockSpec((tm, 128), lambda i: (i, 0))],
        out_specs=[pl.BlockSpec((tm, 128), lambda i: (i, 0)),
                   pl.BlockSpec((tm, 128), lambda i: (i, 0)),
                   pl.BlockSpec((8, 128), lambda i: (0, 0))],
        out_shape=[jax.ShapeDtypeStruct((n, 128), jnp.int32), jax.ShapeDtypeStruct((n, 128), F32),
                   jax.ShapeDtypeStruct((8, 128), F32)],
        scratch_shapes=[pltpu.VMEM((8, 128), F32)],
        compiler_params=_cparams(("arbitrary",)),
        name="router_top2",
    )(logits)


def _gather_rows(src_ref, idx_ref, base, buf, sem, tt):
    def copy(r):
        return pltpu.make_async_copy(src_ref.at[pl.ds(idx_ref[base + r], 1)], buf.at[pl.ds(r, 1)], sem)

    def start(r, carry):
        copy(r).start()
        return carry

    def wait(r, carry):
        copy(r).wait()
        return carry

    lax.fori_loop(0, tt, start, 0, unroll=8)
    lax.fori_loop(0, tt, wait, 0, unroll=8)


def _dispatch_kernel(tok_ref, nt_ref, h_ref, xs_ref, buf, sem, *, tt):
    i = pl.program_id(0)

    @pl.when(i < nt_ref[0])
    def _():
        _gather_rows(h_ref, tok_ref, i * tt, buf, sem, tt)
        xs_ref[...] = buf[...].astype(xs_ref.dtype)

    @pl.when(i >= nt_ref[0])
    def _():
        xs_ref[...] = jnp.zeros_like(xs_ref)


def _dispatch(h, row_token, n_live, tt):
    n, d = h.shape
    n_rows = row_token.shape[0]
    kern = functools.partial(_dispatch_kernel, tt=tt)
    return pl.pallas_call(
        kern,
        grid_spec=pltpu.PrefetchScalarGridSpec(
            num_scalar_prefetch=2,
            grid=(n_rows // tt,),
            in_specs=[pl.BlockSpec(memory_space=pl.ANY)],
            out_specs=pl.BlockSpec((tt, d), lambda i, tok, nt: (i, 0)),
            scratch_shapes=[pltpu.VMEM((tt, d), F32), pltpu.SemaphoreType.DMA(())]),
        out_shape=jax.ShapeDtypeStruct((n_rows, d), BF16),
        compiler_params=_cparams(("arbitrary",)),
        name="moe_dispatch",
    )(row_token, n_live, h)


def _moe_kernel(te_ref, nt_ref, xs_ref, wg_ref, wu_ref, wd_ref, ys_ref, *, tf):
    i = pl.program_id(0)
    f = pl.program_id(1)

    @pl.when(f == 0)
    def _():
        ys_ref[...] = jnp.zeros_like(ys_ref)

    @pl.when(i < nt_ref[0])
    def _():
        _swiglu_step(xs_ref[...],
                     lambda c: wg_ref[0, :, c].astype(BF16),
                     lambda c: wu_ref[0, :, c].astype(BF16),
                     lambda c: wd_ref[0, :, c].astype(BF16),
                     tf, ys_ref)


def _moe_experts(xs, tile_expert, n_live, w_gate, w_up, w_down, tm=MOE_TM, tf=MOE_TF):
    n_rows, d = xs.shape
    ff = w_gate.shape[2]
    n_tiles = n_rows // tm
    n_f = ff // tf

    def live_tile(i, nt):
        return jnp.minimum(i, nt[0] - 1)

    def live_f(i, f, nt):
        return jnp.where(i < nt[0], f, n_f - 1)

    return pl.pallas_call(
        functools.partial(_moe_kernel, tf=tf),
        grid_spec=pltpu.PrefetchScalarGridSpec(
            num_scalar_prefetch=2,
            grid=(n_tiles, n_f),
            in_specs=[pl.BlockSpec((tm, d), lambda i, f, te, nt: (live_tile(i, nt), 0),
                                   pipeline_mode=pl.Buffered(1)),
                      pl.BlockSpec((1, d, tf), lambda i, f, te, nt: (te[i], 0, live_f(i, f, nt))),
                      pl.BlockSpec((1, d, tf), lambda i, f, te, nt: (te[i], 0, live_f(i, f, nt))),
                      pl.BlockSpec((1, tf, d), lambda i, f, te, nt: (te[i], live_f(i, f, nt), 0))],
            out_specs=pl.BlockSpec((tm, d), lambda i, f, te, nt: (i, 0), pipeline_mode=pl.Buffered(1))),
        out_shape=jax.ShapeDtypeStruct((n_rows, d), F32),
        compiler_params=_cparams(("arbitrary", "arbitrary")),
        name="moe_experts",
    )(tile_expert, n_live, xs, w_gate, w_up, w_down)


def _combine_kernel(p1_ref, p2_ref, ys_ref, x_ref, g_ref, o_ref, buf1, buf2, sem, *, tt):
    base = pl.program_id(0) * tt

    def copies(t):
        return (pltpu.make_async_copy(ys_ref.at[pl.ds(p1_ref[base + t], 1)], buf1.at[pl.ds(t, 1)], sem),
                pltpu.make_async_copy(ys_ref.at[pl.ds(p2_ref[base + t], 1)], buf2.at[pl.ds(t, 1)], sem))

    def start(t, carry):
        for c in copies(t):
            c.start()
        return carry

    def wait(t, carry):
        for c in copies(t):
            c.wait()
        return carry

    lax.fori_loop(0, tt, start, 0, unroll=8)
    lax.fori_loop(0, tt, wait, 0, unroll=8)
    g = g_ref[...]
    o_ref[...] = x_ref[...] + (buf1[...] * g[:, 0:1] + buf2[...] * g[:, 1:2])


def _combine(ys, pos1, pos2, gates, x, tt=256):
    n, d = x.shape
    kern = functools.partial(_combine_kernel, tt=tt)
    return pl.pallas_call(
        kern,
        grid_spec=pltpu.PrefetchScalarGridSpec(
            num_scalar_prefetch=2,
            grid=(n // tt,),
            in_specs=[pl.BlockSpec(memory_space=pl.ANY),
                      pl.BlockSpec((tt, d), lambda i, p1, p2: (i, 0)),
                      pl.BlockSpec((tt, 128), lambda i, p1, p2: (i, 0))],
            out_specs=pl.BlockSpec((tt, d), lambda i, p1, p2: (i, 0)),
            scratch_shapes=[pltpu.VMEM((tt, d), F32), pltpu.VMEM((tt, d), F32),
                            pltpu.SemaphoreType.DMA(())]),
        out_shape=jax.ShapeDtypeStruct((n, d), F32),
        compiler_params=_cparams(("arbitrary",)),
        name="moe_combine",
    )(pos1, pos2, ys, x, gates)


def _even_layer(x, rel_bias, norm_mix, w_in, q_g, k_g, conv_w, conv_b, cn_g, cn_b, w_out, norm_next,
                bsz, seq):
    n, d = x.shape
    a_width = A_HEADS * HEAD_DIM
    h = _rmsnorm(x, norm_mix)
    w_in = w_in.astype(BF16)
    qkv = _qkv_proj(h, w_in[:, :3 * a_width], q_g, k_g, a_width)
    pc = _matmul(h, w_in[:, 3 * a_width:], name="conv_proj")
    ch = pc.shape[1] // 2
    o_a = _attention(qkv.reshape(bsz, seq, 3 * a_width), rel_bias, bsz, seq).reshape(n, a_width)
    o_b = _conv_module(pc.reshape(bsz, seq, 2 * ch), conv_w, conv_b, cn_g, cn_b).reshape(n, ch)
    w_out = w_out.astype(BF16)
    return _outproj(o_a, o_b, w_out[:a_width], w_out[a_width:], x, norm_next)


def _moe_layer(x, h, logits, w_gate, w_up, w_down):
    n, d = x.shape
    tm = MOE_TM
    oi, og, cnt = _router(logits)
    e1, e2, r1, r2 = oi[:, 0], oi[:, 1], oi[:, 2], oi[:, 3]
    counts = cnt[0, :N_EXPERTS].astype(jnp.int32)
    padded = ((counts + tm - 1) // tm) * tm
    ends = jnp.cumsum(padded)
    starts = ends - padded
    pos1 = starts[e1] + r1
    pos2 = starts[e2] + r2
    n_tiles = (2 * n) // tm + N_EXPERTS
    n_live = (ends[-1] // tm).astype(jnp.int32).reshape(1)
    tile_row = jnp.minimum(jnp.arange(n_tiles, dtype=jnp.int32), n_live[0] - 1) * tm
    tile_expert = jnp.sum(tile_row[:, None] >= ends[None, :], axis=1).astype(jnp.int32)
    tok = jnp.arange(n, dtype=jnp.int32)
    row_token = jnp.zeros((n_tiles * tm,), jnp.int32).at[pos1].set(tok).at[pos2].set(tok)
    xs = _dispatch(h, row_token, n_live, tm)
    ys = _moe_experts(xs, tile_expert, n_live, w_gate, w_up, w_down)
    return _combine(ys, pos1, pos2, og, x)


def kernel(x, rel_bias, even_norm_mix, even_w_in, even_q_norm, even_k_norm, even_conv_w, even_conv_b, even_cnorm_g, even_cnorm_b, even_w_out, even_norm_ffn, even_ffn_w1, even_ffn_w3, even_ffn_w2, odd_norm_mix, odd_w_u, odd_b_u, odd_vnorm_g, odd_vnorm_b, odd_w_s, odd_b_s, odd_w_o, odd_norm_ffn, odd_router, odd_we_gate, odd_we_up, odd_we_down):
    bsz, seq, d = x.shape
    xf = x.reshape(bsz * seq, d)
    xf, h = _even_layer(xf, rel_bias, even_norm_mix[0], even_w_in[0], even_q_norm[0], even_k_norm[0],
                        even_conv_w[0], even_conv_b[0], even_cnorm_g[0], even_cnorm_b[0], even_w_out[0],
                        even_norm_ffn[0], bsz, seq)
    xf, h = _ffn(h, even_ffn_w1[0].astype(BF16), even_ffn_w3[0].astype(BF16), even_ffn_w2[0].astype(BF16),
                 xf, odd_norm_mix[0])
    z = _matmul(h, odd_w_u[0].astype(BF16), bias=odd_b_u[0], name="gmlp_in_proj")
    xf, h, logits = _gate_outproj(z, odd_w_s[0], odd_b_s[0], odd_vnorm_g[0], odd_vnorm_b[0],
                                  odd_w_o[0].astype(BF16), xf, odd_norm_ffn[0], odd_router[0])
    xf = _moe_layer(xf, h, logits, odd_we_gate[0], odd_we_up[0], odd_we_down[0])
    return xf.reshape(bsz, seq, d)
```

```python
import functools
import math

import numpy as np
import jax
import jax.numpy as jnp
from jax import lax
from jax.experimental import pallas as pl
from jax.experimental.pallas import tpu as pltpu

F32 = jnp.float32
BF16 = jnp.bfloat16
EPS = 1e-6

HEAD_DIM = 128
A_HEADS = 8
DILATED_BRANCHES = ((128, 1), (512, 4), (2048, 16))
ATT_BLK = 128
REL_BUCKETS = 32
REL_MAX_DIST = 2048
CONV_WIDTH = 31
CONV_HALO = 32
CHUNK = 128
GMLP_GROUPS = 16
N_EXPERTS = 8
NEG = -1e30

VMEM_LIMIT_BYTES = 56 * 1024 * 1024
MOE_TM = 1024
MOE_TF = 512
MOE_ROW_BLOCK = 256
SWIGLU_UP_CHUNK = 256
SWIGLU_DOWN_CHUNK = 512


def _cparams(sem):
    return pltpu.CompilerParams(dimension_semantics=sem, vmem_limit_bytes=VMEM_LIMIT_BYTES)


def _rms(x, g):
    ms = jnp.mean(x * x, axis=-1, keepdims=True)
    return x * lax.rsqrt(ms + EPS) * g


def _rmsnorm_kernel(x_ref, g_ref, o_ref):
    o_ref[...] = _rms(x_ref[...], g_ref[...]).astype(o_ref.dtype)


def _rmsnorm(x, g, tm=512):
    n, d = x.shape
    return pl.pallas_call(
        _rmsnorm_kernel,
        grid=(n // tm,),
        in_specs=[pl.BlockSpec((tm, d), lambda i: (i, 0)),
                  pl.BlockSpec((1, d), lambda i: (0, 0))],
        out_specs=pl.BlockSpec((tm, d), lambda i: (i, 0)),
        out_shape=jax.ShapeDtypeStruct((n, d), BF16),
        compiler_params=_cparams(("parallel",)),
        name="rmsnorm",
    )(x, g.reshape(1, d))


def _qkv_kernel(h_ref, w_ref, qg_ref, kg_ref, o_ref, *, tn, q_tiles, k_tiles, q_scale):
    j = pl.program_id(1)
    acc = jnp.dot(h_ref[...], w_ref[...], preferred_element_type=F32)

    def head_norm(g, scale):
        for hd in range(tn // HEAD_DIM):
            sl = slice(hd * HEAD_DIM, (hd + 1) * HEAD_DIM)
            o_ref[:, sl] = _rms(acc[:, sl], g) * scale

    @pl.when(j < q_tiles)
    def _():
        head_norm(qg_ref[...], q_scale)

    @pl.when(jnp.logical_and(j >= q_tiles, j < q_tiles + k_tiles))
    def _():
        head_norm(kg_ref[...], 1.0)

    @pl.when(j >= q_tiles + k_tiles)
    def _():
        o_ref[...] = acc


def _qkv_proj(h, w, q_g, k_g, a_width, tm=1024, tn=1024):
    n, d = h.shape
    nout = w.shape[1]
    kern = functools.partial(_qkv_kernel, tn=tn, q_tiles=a_width // tn, k_tiles=a_width // tn,
                             q_scale=HEAD_DIM ** -0.5)
    return pl.pallas_call(
        kern,
        grid=(n // tm, nout // tn),
        in_specs=[pl.BlockSpec((tm, d), lambda i, j: (i, 0)),
                  pl.BlockSpec((d, tn), lambda i, j: (0, j)),
                  pl.BlockSpec((1, HEAD_DIM), lambda i, j: (0, 0)),
                  pl.BlockSpec((1, HEAD_DIM), lambda i, j: (0, 0))],
        out_specs=pl.BlockSpec((tm, tn), lambda i, j: (i, j)),
        out_shape=jax.ShapeDtypeStruct((n, nout), F32),
        compiler_params=_cparams(("parallel", "parallel")),
        name="qkv_proj",
    )(h, w, q_g.reshape(1, HEAD_DIM), k_g.reshape(1, HEAD_DIM))


def _gelu_tanh(x):
    c = math.sqrt(2.0 / math.pi)
    return 0.5 * x * (1.0 + jnp.tanh(c * (x + 0.044715 * (x * x * x))))


def _mm_kernel(h_ref, w_ref, o_ref):
    o_ref[...] = jnp.dot(h_ref[...], w_ref[...], preferred_element_type=F32).astype(o_ref.dtype)


def _mm_bias_gelu_kernel(h_ref, w_ref, b_ref, o_ref):
    acc = jnp.dot(h_ref[...], w_ref[...], preferred_element_type=F32) + b_ref[...]
    o_ref[...] = _gelu_tanh(acc).astype(o_ref.dtype)


def _matmul(h, w, bias=None, tm=1024, tn=1024, name="matmul"):
    n, d = h.shape
    m = w.shape[1]
    in_specs = [pl.BlockSpec((tm, d), lambda i, j: (i, 0)),
                pl.BlockSpec((d, tn), lambda i, j: (0, j))]
    args = [h, w]
    kern = _mm_kernel
    if bias is not None:
        in_specs.append(pl.BlockSpec((1, tn), lambda i, j: (0, j)))
        args.append(bias.reshape(1, m))
        kern = _mm_bias_gelu_kernel
    return pl.pallas_call(
        kern,
        grid=(n // tm, m // tn),
        in_specs=in_specs,
        out_specs=pl.BlockSpec((tm, tn), lambda i, j: (i, j)),
        out_shape=jax.ShapeDtypeStruct((n, m), BF16),
        compiler_params=_cparams(("parallel", "parallel")),
        name=name,
    )(*args)


def _t5_bucket_np(dist):
    max_exact = REL_BUCKETS // 2
    d = np.maximum(dist, 0)
    log_ratio = (np.log(np.maximum(d, 1).astype(np.float32) / np.float32(max_exact))
                 / np.float32(math.log(REL_MAX_DIST / max_exact)))
    large = max_exact + (log_ratio.astype(np.float32) * np.float32(REL_BUCKETS - max_exact)).astype(np.int32)
    large = np.minimum(large, REL_BUCKETS - 1)
    return np.where(d < max_exact, d, large).astype(np.int32)


def _branch_bucket_tables():
    blk = ATT_BLK
    qi = np.arange(blk)[:, None]
    kj = np.arange(2 * blk)[None, :]
    dm = qi + blk - kj
    tabs = []
    for _, dil in DILATED_BRANCHES:
        bucket = _t5_bucket_np(dm * dil)
        tabs.append(np.where((dm >= 0) & (dm <= blk), bucket, -1))
    return np.stack(tabs).astype(np.int32)


def _attn_kernel(rb_ref, bidx_ref, q_ref, k_ref, v_ref, o_ref, bias_scr, ob_scr, m_scr, l_scr, *, seq):
    blk = ATT_BLK
    hidx = pl.program_id(0)

    @pl.when(pl.program_id(1) == 0)
    def _():
        for br in range(len(DILATED_BRANCHES)):
            idx = bidx_ref[br]
            bias = jnp.full(idx.shape, NEG, F32)
            for u in range(REL_BUCKETS):
                bias = jnp.where(idx == u, rb_ref[u, hidx], bias)
            bias_scr[br] = bias

    def block(br, q_rows, k_rows, bias):
        qb = q_ref[0, q_rows, :].astype(BF16)
        kb = k_ref[0, k_rows, :].astype(BF16)
        vb = v_ref[0, k_rows, :].astype(BF16)
        s = lax.dot_general(qb, kb, (((1,), (1,)), ((), ())), preferred_element_type=F32) + bias
        m = jnp.max(s, axis=-1, keepdims=True)
        p = jnp.exp(s - m)
        l = jnp.sum(p, axis=-1, keepdims=True)
        o = jnp.dot(p.astype(BF16), vb, preferred_element_type=F32)
        ob_scr[br, q_rows, :] = o
        m_scr[br, q_rows, :] = jnp.broadcast_to(m, (blk, HEAD_DIM))
        l_scr[br, q_rows, :] = jnp.broadcast_to(l, (blk, HEAD_DIM))

    for br, (window, dil) in enumerate(DILATED_BRANCHES):
        n_pos = seq // dil
        nb = -(-n_pos // blk)
        for r in range(dil):
            for nblk in range(nb):
                start = r + dil * blk * nblk

                def rows(first, count):
                    return pl.ds(first, count) if dil == 1 else pl.ds(first, count, stride=dil)

                q_rows = rows(start, blk)
                if nblk == 0:
                    block(br, q_rows, q_rows, bias_scr[br, :, blk:])
                else:
                    block(br, q_rows, rows(start - dil * blk, 2 * blk), bias_scr[br])

    n_br = len(DILATED_BRANCHES)
    m_all = m_scr[0]
    for br in range(1, n_br):
        m_all = jnp.maximum(m_all, m_scr[br])
    num = jnp.zeros((seq, HEAD_DIM), F32)
    den = jnp.zeros((seq, HEAD_DIM), F32)
    for br in range(n_br):
        a = jnp.exp(m_scr[br] - m_all)
        num = num + a * ob_scr[br]
        den = den + a * l_scr[br]
    o_ref[0] = (num / den).astype(o_ref.dtype)


def _attention(qkv, rel_bias, bsz, seq):
    h, e = A_HEADS, HEAD_DIM
    n_br = len(DILATED_BRANCHES)
    bidx = jnp.asarray(_branch_bucket_tables())
    kern = functools.partial(_attn_kernel, seq=seq)
    return pl.pallas_call(
        kern,
        grid=(h, bsz),
        in_specs=[pl.BlockSpec(memory_space=pltpu.SMEM),
                  pl.BlockSpec((n_br, ATT_BLK, 2 * ATT_BLK), lambda hh, b: (0, 0, 0)),
                  pl.BlockSpec((1, seq, e), lambda hh, b: (b, 0, hh)),
                  pl.BlockSpec((1, seq, e), lambda hh, b: (b, 0, h + hh)),
                  pl.BlockSpec((1, seq, e), lambda hh, b: (b, 0, 2 * h + hh))],
        out_specs=pl.BlockSpec((1, seq, e), lambda hh, b: (b, 0, hh)),
        out_shape=jax.ShapeDtypeStruct((bsz, seq, h * e), BF16),
        scratch_shapes=[pltpu.VMEM((n_br, ATT_BLK, 2 * ATT_BLK), F32),
                        pltpu.VMEM((n_br, seq, e), F32),
                        pltpu.VMEM((n_br, seq, e), F32),
                        pltpu.VMEM((n_br, seq, e), F32)],
        compiler_params=_cparams(("arbitrary", "arbitrary")),
        name="dilated_attention",
    )(rel_bias, bidx, qkv, qkv, qkv)


def _conv_kernel(cv_ref, cg_ref, hv_ref, hg_ref, w_ref, b_ref, lg_ref, lb_ref, o_ref, g_scr, y_scr, *, tt):
    halo = CONV_HALO
    t = pl.program_id(1)
    hv = hv_ref[0].astype(F32)
    hg = hg_ref[0].astype(F32)
    g_scr[0:halo, :] = jnp.where(t > 0, hv * jax.nn.sigmoid(hg), 0.0)
    cv = cv_ref[0].astype(F32)
    cg = cg_ref[0].astype(F32)
    g_scr[halo:, :] = cv * jax.nn.sigmoid(cg)

    ch = g_scr.shape[1]
    rc, cc = 32, 256
    first = halo - (CONV_WIDTH - 1)
    for r0 in range(0, tt, rc):
        for c0 in range(0, ch, cc):
            acc = jnp.broadcast_to(b_ref[:, c0:c0 + cc], (rc, cc))
            for k in range(CONV_WIDTH):
                acc = acc + w_ref[k:k + 1, c0:c0 + cc] * g_scr[r0 + first + k:r0 + first + k + rc, c0:c0 + cc]
            y_scr[r0:r0 + rc, c0:c0 + cc] = acc

    y = y_scr[...]
    mu = jnp.mean(y, axis=-1, keepdims=True)
    yc = y - mu
    var = jnp.mean(yc * yc, axis=-1, keepdims=True)
    z = yc * lax.rsqrt(var + EPS) * lg_ref[...] + lb_ref[...]
    o_ref[0] = (z * jax.nn.sigmoid(z)).astype(o_ref.dtype)


def _conv_module(pc, conv_w, conv_b, ln_g, ln_b, tt=256):
    bsz, seq, ch2 = pc.shape
    ch = ch2 // 2
    hb = tt // CONV_HALO
    kern = functools.partial(_conv_kernel, tt=tt)
    vec = lambda a: a.reshape(1, ch)
    return pl.pallas_call(
        kern,
        grid=(bsz, seq // tt),
        in_specs=[pl.BlockSpec((1, tt, ch), lambda b, t: (b, t, 0)),
                  pl.BlockSpec((1, tt, ch), lambda b, t: (b, t, 1)),
                  pl.BlockSpec((1, CONV_HALO, ch), lambda b, t: (b, jnp.maximum(t * hb - 1, 0), 0)),
                  pl.BlockSpec((1, CONV_HALO, ch), lambda b, t: (b, jnp.maximum(t * hb - 1, 0), 1)),
                  pl.BlockSpec((CONV_WIDTH, ch), lambda b, t: (0, 0)),
                  pl.BlockSpec((1, ch), lambda b, t: (0, 0)),
                  pl.BlockSpec((1, ch), lambda b, t: (0, 0)),
                  pl.BlockSpec((1, ch), lambda b, t: (0, 0))],
        out_specs=pl.BlockSpec((1, tt, ch), lambda b, t: (b, t, 0)),
        out_shape=jax.ShapeDtypeStruct((bsz, seq, ch), BF16),
        scratch_shapes=[pltpu.VMEM((tt + CONV_HALO, ch), F32), pltpu.VMEM((tt, ch), F32)],
        compiler_params=_cparams(("parallel", "parallel")),
        name="conv_module",
    )(pc, pc, pc, pc, conv_w, vec(conv_b), vec(ln_g), vec(ln_b))


def _outproj_kernel(a_ref, b_ref, wa_ref, wb_ref, x_ref, g_ref, xo_ref, ho_ref):
    acc = jnp.dot(a_ref[...], wa_ref[...], preferred_element_type=F32)
    acc = acc + jnp.dot(b_ref[...], wb_ref[...], preferred_element_type=F32)
    xn = x_ref[...] + acc
    xo_ref[...] = xn
    ho_ref[...] = _rms(xn, g_ref[...]).astype(ho_ref.dtype)


def _outproj(a, b, wa, wb, x, g, tm=512):
    n, d = x.shape
    ka, kb = a.shape[1], b.shape[1]
    return pl.pallas_call(
        _outproj_kernel,
        grid=(n // tm,),
        in_specs=[pl.BlockSpec((tm, ka), lambda i: (i, 0)),
                  pl.BlockSpec((tm, kb), lambda i: (i, 0)),
                  pl.BlockSpec((ka, d), lambda i: (0, 0)),
                  pl.BlockSpec((kb, d), lambda i: (0, 0)),
                  pl.BlockSpec((tm, d), lambda i: (i, 0)),
                  pl.BlockSpec((1, d), lambda i: (0, 0))],
        out_specs=[pl.BlockSpec((tm, d), lambda i: (i, 0)),
                   pl.BlockSpec((tm, d), lambda i: (i, 0))],
        out_shape=[jax.ShapeDtypeStruct((n, d), F32), jax.ShapeDtypeStruct((n, d), BF16)],
        compiler_params=_cparams(("parallel",)),
        name="out_proj",
    )(a, b, wa, wb, x, g.reshape(1, d))


def _swiglu_step(x, load_gate, load_up, load_down, tf, acc_ref, rows=slice(None)):
    hid = []
    for c0 in range(0, tf, SWIGLU_UP_CHUNK):
        cols = slice(c0, c0 + SWIGLU_UP_CHUNK)
        a = jnp.dot(x, load_gate(cols), preferred_element_type=F32)
        b = jnp.dot(x, load_up(cols), preferred_element_type=F32)
        hid.append((a * jax.nn.sigmoid(a) * b).astype(BF16))
    hid = jnp.concatenate(hid, axis=1)
    for n0 in range(0, acc_ref.shape[1], SWIGLU_DOWN_CHUNK):
        cols = slice(n0, n0 + SWIGLU_DOWN_CHUNK)
        acc_ref[rows, cols] += jnp.dot(hid, load_down(cols), preferred_element_type=F32)


def _ffn_kernel(h_ref, w1_ref, w3_ref, w2_ref, x_ref, g_ref, xo_ref, ho_ref, *, tf):
    f = pl.program_id(1)

    @pl.when(f == 0)
    def _():
        xo_ref[...] = jnp.zeros_like(xo_ref)

    _swiglu_step(h_ref[...], lambda c: w1_ref[:, c], lambda c: w3_ref[:, c], lambda c: w2_ref[:, c],
                 tf, xo_ref)

    @pl.when(f == pl.num_programs(1) - 1)
    def _():
        xn = x_ref[...] + xo_ref[...]
        xo_ref[...] = xn
        ho_ref[...] = _rms(xn, g_ref[...]).astype(ho_ref.dtype)


def _ffn(h, w1, w3, w2, x, g, tm=512, tf=512):
    n, d = x.shape
    ff = w1.shape[1]
    return pl.pallas_call(
        functools.partial(_ffn_kernel, tf=tf),
        grid=(n // tm, ff // tf),
        in_specs=[pl.BlockSpec((tm, d), lambda i, f: (i, 0)),
                  pl.BlockSpec((d, tf), lambda i, f: (0, f)),
                  pl.BlockSpec((d, tf), lambda i, f: (0, f)),
                  pl.BlockSpec((tf, d), lambda i, f: (f, 0)),
                  pl.BlockSpec((tm, d), lambda i, f: (i, 0)),
                  pl.BlockSpec((1, d), lambda i, f: (0, 0))],
        out_specs=[pl.BlockSpec((tm, d), lambda i, f: (i, 0)),
                   pl.BlockSpec((tm, d), lambda i, f: (i, 0))],
        out_shape=[jax.ShapeDtypeStruct((n, d), F32), jax.ShapeDtypeStruct((n, d), BF16)],
        compiler_params=_cparams(("parallel", "arbitrary")),
        name="dense_swiglu",
    )(h, w1, w3, w2, x, g.reshape(1, d))


def _gate_kernel(z_ref, ws_ref, bs_ref, vg_ref, vb_ref, wo_ref, x_ref, g_ref, wr_ref,
                 xo_ref, ho_ref, lg_ref, gated_scr, *, tm, width):
    gch = width // GMLP_GROUPS
    row = lax.broadcasted_iota(jnp.int32, (CHUNK, CHUNK), 0)
    col = lax.broadcasted_iota(jnp.int32, (CHUNK, CHUNK), 1)
    causal = row >= col
    for c0 in range(0, tm, CHUNK):
        v = z_ref[c0:c0 + CHUNK, width:].astype(F32)
        mu = jnp.mean(v, axis=-1, keepdims=True)
        vc = v - mu
        var = jnp.mean(vc * vc, axis=-1, keepdims=True)
        vn = (vc * lax.rsqrt(var + EPS) * vg_ref[...] + vb_ref[...]).astype(BF16)
        for gi in range(GMLP_GROUPS):
            cs = slice(gi * gch, (gi + 1) * gch)
            wsg = jnp.where(causal, ws_ref[gi], 0.0).astype(BF16)
            sv = jnp.dot(wsg, vn[:, cs], preferred_element_type=F32) + bs_ref[:, cs]
            u = z_ref[c0:c0 + CHUNK, cs].astype(F32)
            gated_scr[c0:c0 + CHUNK, cs] = (u * sv).astype(BF16)
    xn = x_ref[...] + jnp.dot(gated_scr[...], wo_ref[...], preferred_element_type=F32)
    xo_ref[...] = xn
    hn = _rms(xn, g_ref[...])
    ho_ref[...] = hn.astype(ho_ref.dtype)
    lg_ref[...] = jnp.dot(hn, wr_ref[...], preferred_element_type=F32, precision=lax.Precision.HIGHEST)


def _gate_outproj(z, w_s, b_s, vn_g, vn_b, w_o, x, g, w_router, tm=512):
    n, d = x.shape
    width = z.shape[1] // 2
    gch = width // GMLP_GROUPS
    bs_x = jnp.repeat(b_s.T, gch, axis=1)
    wr = jnp.zeros((d, 128), F32).at[:, :N_EXPERTS].set(w_router)
    kern = functools.partial(_gate_kernel, tm=tm, width=width)
    full = lambda shape: pl.BlockSpec(shape, lambda i: (0,) * len(shape))
    return pl.pallas_call(
        kern,
        grid=(n // tm,),
        in_specs=[pl.BlockSpec((tm, 2 * width), lambda i: (i, 0)),
                  full((GMLP_GROUPS, CHUNK, CHUNK)),
                  full((CHUNK, width)),
                  full((1, width)),
                  full((1, width)),
                  full((width, d)),
                  pl.BlockSpec((tm, d), lambda i: (i, 0)),
                  full((1, d)),
                  full((d, 128))],
        out_specs=[pl.BlockSpec((tm, d), lambda i: (i, 0)),
                   pl.BlockSpec((tm, d), lambda i: (i, 0)),
                   pl.BlockSpec((tm, 128), lambda i: (i, 0))],
        out_shape=[jax.ShapeDtypeStruct((n, d), F32), jax.ShapeDtypeStruct((n, d), F32),
                   jax.ShapeDtypeStruct((n, 128), F32)],
        scratch_shapes=[pltpu.VMEM((tm, width), BF16)],
        compiler_params=_cparams(("parallel",)),
        name="gmlp_gate_outproj",
    )(z, w_s, bs_x, vn_g.reshape(1, width), vn_b.reshape(1, width), w_o, x, g.reshape(1, d), wr)


def _router_kernel(lg_ref, oi_ref, og_ref, cnt_ref, carry_scr, *, tm):
    i = pl.program_id(0)

    @pl.when(i == 0)
    def _():
        carry_scr[...] = jnp.zeros_like(carry_scr)

    lane = lax.broadcasted_iota(jnp.int32, (tm, 128), 1)
    lg = jnp.where(lane < N_EXPERTS, lg_ref[...], -jnp.inf)
    m1 = jnp.max(lg, axis=-1, keepdims=True)
    i1 = jnp.min(jnp.where(lg == m1, lane, 128), axis=-1, keepdims=True)
    lg2 = jnp.where(lane == i1, -jnp.inf, lg)
    m2 = jnp.max(lg2, axis=-1, keepdims=True)
    i2 = jnp.min(jnp.where(lg2 == m2, lane, 128), axis=-1, keepdims=True)
    e2 = jnp.exp(m2 - m1)
    den = 1.0 + e2
    g1 = 1.0 / den
    g2 = e2 / den

    sel = jnp.logical_or(lane == i1, lane == i2)
    row = lax.broadcasted_iota(jnp.int32, (tm, tm), 0)
    col = lax.broadcasted_iota(jnp.int32, (tm, tm), 1)
    before = jnp.where(row > col, 1.0, 0.0).astype(BF16)
    selb = jnp.where(sel, 1.0, 0.0)
    tot = jnp.dot(before, selb.astype(BF16), preferred_element_type=F32) + carry_scr[0:1, :]
    r1 = jnp.sum(jnp.where(lane == i1, tot, 0.0), axis=-1, keepdims=True).astype(jnp.int32)
    r2 = jnp.sum(jnp.where(lane == i2, tot, 0.0), axis=-1, keepdims=True).astype(jnp.int32)
    new_carry = carry_scr[0:1, :] + jnp.sum(selb, axis=0, keepdims=True)
    carry_scr[...] = jnp.broadcast_to(new_carry, carry_scr.shape)
    cnt_ref[...] = jnp.broadcast_to(new_carry, cnt_ref.shape)

    oi_ref[...] = jnp.where(lane == 0, i1, jnp.where(lane == 1, i2, jnp.where(lane == 2, r1, r2)))
    og_ref[...] = jnp.where(lane == 0, g1, g2)


def _router(logits, tm=512):
    n = logits.shape[0]
    kern = functools.partial(_router_kernel, tm=tm)
    return pl.pallas_call(
        kern,
        grid=(n // tm,),
        in_specs=[pl.BlockSpec((tm, 128), lambda i: (i, 0))],
        out_specs=[pl.BlockSpec((tm, 128), lambda i: (i, 0)),
                   pl.BlockSpec((tm, 128), lambda i: (i, 0)),
                   pl.BlockSpec((8, 128), lambda i: (0, 0))],
        out_shape=[jax.ShapeDtypeStruct((n, 128), jnp.int32), jax.ShapeDtypeStruct((n, 128), F32),
                   jax.ShapeDtypeStruct((8, 128), F32)],
        scratch_shapes=[pltpu.VMEM((8, 128), F32)],
        compiler_params=_cparams(("arbitrary",)),
        name="router_top2",
    )(logits)


def _gather_rows(src_ref, idx_ref, base, buf, sem, tt):
    def copy(r):
        return pltpu.make_async_copy(src_ref.at[pl.ds(idx_ref[base + r], 1)], buf.at[pl.ds(r, 1)], sem)

    def start(r, carry):
        copy(r).start()
        return carry

    def wait(r, carry):
        copy(r).wait()
        return carry

    lax.fori_loop(0, tt, start, 0, unroll=8)
    lax.fori_loop(0, tt, wait, 0, unroll=8)


def _dispatch_kernel(tok_ref, nt_ref, h_ref, xs_ref, buf, sem, *, tt):
    i = pl.program_id(0)

    @pl.when(i < nt_ref[0])
    def _():
        _gather_rows(h_ref, tok_ref, i * tt, buf, sem, tt)
        xs_ref[...] = buf[...].astype(xs_ref.dtype)

    @pl.when(i >= nt_ref[0])
    def _():
        xs_ref[...] = jnp.zeros_like(xs_ref)


def _dispatch(h, row_token, n_live, tt):
    n, d = h.shape
    n_rows = row_token.shape[0]
    kern = functools.partial(_dispatch_kernel, tt=tt)
    return pl.pallas_call(
        kern,
        grid_spec=pltpu.PrefetchScalarGridSpec(
            num_scalar_prefetch=2,
            grid=(n_rows // tt,),
            in_specs=[pl.BlockSpec(memory_space=pl.ANY)],
            out_specs=pl.BlockSpec((tt, d), lambda i, tok, nt: (i, 0)),
            scratch_shapes=[pltpu.VMEM((tt, d), F32), pltpu.SemaphoreType.DMA(())]),
        out_shape=jax.ShapeDtypeStruct((n_rows, d), BF16),
        compiler_params=_cparams(("arbitrary",)),
        name="moe_dispatch",
    )(row_token, n_live, h)


def _moe_kernel(te_ref, nt_ref, tb_ref, xs_ref, wg_ref, wu_ref, wd_ref, ys_ref, *, tf, tm):
    del te_ref, nt_ref
    i = pl.program_id(0)
    f = pl.program_id(1)

    @pl.when(f == 0)
    def _():
        ys_ref[...] = jnp.zeros_like(ys_ref)

    for k in range(1, tm // MOE_ROW_BLOCK + 1):
        @pl.when(tb_ref[i] == k)
        def _(k=k):
            rows = slice(0, k * MOE_ROW_BLOCK)
            _swiglu_step(xs_ref[rows, :],
                         lambda c: wg_ref[0, :, c].astype(BF16),
                         lambda c: wu_ref[0, :, c].astype(BF16),
                         lambda c: wd_ref[0, :, c].astype(BF16),
                         tf, ys_ref, rows)


def _moe_experts(xs, tile_expert, n_live, tile_blocks, w_gate, w_up, w_down, tm=MOE_TM, tf=MOE_TF):
    n_rows, d = xs.shape
    ff = w_gate.shape[2]
    n_tiles = n_rows // tm
    n_f = ff // tf

    def live_tile(i, nt):
        return jnp.minimum(i, nt[0] - 1)

    def live_f(i, f, nt):
        return jnp.where(i < nt[0], f, n_f - 1)

    return pl.pallas_call(
        functools.partial(_moe_kernel, tf=tf, tm=tm),
        grid_spec=pltpu.PrefetchScalarGridSpec(
            num_scalar_prefetch=3,
            grid=(n_tiles, n_f),
            in_specs=[pl.BlockSpec((tm, d), lambda i, f, te, nt, tb: (live_tile(i, nt), 0),
                                   pipeline_mode=pl.Buffered(1)),
                      pl.BlockSpec((1, d, tf), lambda i, f, te, nt, tb: (te[i], 0, live_f(i, f, nt))),
                      pl.BlockSpec((1, d, tf), lambda i, f, te, nt, tb: (te[i], 0, live_f(i, f, nt))),
                      pl.BlockSpec((1, tf, d), lambda i, f, te, nt, tb: (te[i], live_f(i, f, nt), 0))],
            out_specs=pl.BlockSpec((tm, d), lambda i, f, te, nt, tb: (i, 0), pipeline_mode=pl.Buffered(1))),
        out_shape=jax.ShapeDtypeStruct((n_rows, d), F32),
        compiler_params=_cparams(("arbitrary", "arbitrary")),
        name="moe_experts",
    )(tile_expert, n_live, tile_blocks, xs, w_gate, w_up, w_down)


def _combine_kernel(p1_ref, p2_ref, ys_ref, x_ref, g_ref, o_ref, buf1, buf2, sem, *, tt):
    base = pl.program_id(0) * tt

    def copies(t):
        return (pltpu.make_async_copy(ys_ref.at[pl.ds(p1_ref[base + t], 1)], buf1.at[pl.ds(t, 1)], sem),
                pltpu.make_async_copy(ys_ref.at[pl.ds(p2_ref[base + t], 1)], buf2.at[pl.ds(t, 1)], sem))

    def start(t, carry):
        for c in copies(t):
            c.start()
        return carry

    def wait(t, carry):
        for c in copies(t):
            c.wait()
        return carry

    lax.fori_loop(0, tt, start, 0, unroll=8)
    lax.fori_loop(0, tt, wait, 0, unroll=8)
    g = g_ref[...]
    o_ref[...] = x_ref[...] + (buf1[...] * g[:, 0:1] + buf2[...] * g[:, 1:2])


def _combine(ys, pos1, pos2, gates, x, tt=256):
    n, d = x.shape
    kern = functools.partial(_combine_kernel, tt=tt)
    return pl.pallas_call(
        kern,
        grid_spec=pltpu.PrefetchScalarGridSpec(
            num_scalar_prefetch=2,
            grid=(n // tt,),
            in_specs=[pl.BlockSpec(memory_space=pl.ANY),
                      pl.BlockSpec((tt, d), lambda i, p1, p2: (i, 0)),
                      pl.BlockSpec((tt, 128), lambda i, p1, p2: (i, 0))],
            out_specs=pl.BlockSpec((tt, d), lambda i, p1, p2: (i, 0)),
            scratch_shapes=[pltpu.VMEM((tt, d), F32), pltpu.VMEM((tt, d), F32),
                            pltpu.SemaphoreType.DMA(())]),
        out_shape=jax.ShapeDtypeStruct((n, d), F32),
        compiler_params=_cparams(("arbitrary",)),
        name="moe_combine",
    )(pos1, pos2, ys, x, gates)


def _even_layer(x, rel_bias, norm_mix, w_in, q_g, k_g, conv_w, conv_b, cn_g, cn_b, w_out, norm_next,
                bsz, seq):
    n, d = x.shape
    a_width = A_HEADS * HEAD_DIM
    h = _rmsnorm(x, norm_mix)
    w_in = w_in.astype(BF16)
    qkv = _qkv_proj(h, w_in[:, :3 * a_width], q_g, k_g, a_width)
    pc = _matmul(h, w_in[:, 3 * a_width:], name="conv_proj")
    ch = pc.shape[1] // 2
    o_a = _attention(qkv.reshape(bsz, seq, 3 * a_width), rel_bias, bsz, seq).reshape(n, a_width)
    o_b = _conv_module(pc.reshape(bsz, seq, 2 * ch), conv_w, conv_b, cn_g, cn_b).reshape(n, ch)
    w_out = w_out.astype(BF16)
    return _outproj(o_a, o_b, w_out[:a_width], w_out[a_width:], x, norm_next)


def _moe_layer(x, h, logits, w_gate, w_up, w_down):
    n, d = x.shape
    tm = MOE_TM
    oi, og, cnt = _router(logits)
    e1, e2, r1, r2 = oi[:, 0], oi[:, 1], oi[:, 2], oi[:, 3]
    counts = cnt[0, :N_EXPERTS].astype(jnp.int32)
    padded = ((counts + tm - 1) // tm) * tm
    ends = jnp.cumsum(padded)
    starts = ends - padded
    pos1 = starts[e1] + r1
    pos2 = starts[e2] + r2
    n_tiles = (2 * n) // tm + N_EXPERTS
    n_live = (ends[-1] // tm).astype(jnp.int32).reshape(1)
    tile_row = jnp.minimum(jnp.arange(n_tiles, dtype=jnp.int32), n_live[0] - 1) * tm
    tile_expert = jnp.sum(tile_row[:, None] >= ends[None, :], axis=1).astype(jnp.int32)
    tok = jnp.arange(n, dtype=jnp.int32)
    row_token = jnp.zeros((n_tiles * tm,), jnp.int32).at[jnp.concatenate([pos1, pos2])].set(
        jnp.concatenate([tok, tok]))
    tile_live = jnp.arange(n_tiles, dtype=jnp.int32) < n_live[0]
    tile_rows = jnp.clip((starts + counts)[tile_expert] - tile_row, 0, tm)
    tile_blocks = jnp.where(tile_live, (tile_rows + MOE_ROW_BLOCK - 1) // MOE_ROW_BLOCK, 0).astype(jnp.int32)
    xs = _dispatch(h, row_token, n_live, tm)
    ys = _moe_experts(xs, tile_expert, n_live, tile_blocks, w_gate, w_up, w_down)
    return _combine(ys, pos1, pos2, og, x)


def kernel(x, rel_bias, even_norm_mix, even_w_in, even_q_norm, even_k_norm, even_conv_w, even_conv_b, even_cnorm_g, even_cnorm_b, even_w_out, even_norm_ffn, even_ffn_w1, even_ffn_w3, even_ffn_w2, odd_norm_mix, odd_w_u, odd_b_u, odd_vnorm_g, odd_vnorm_b, odd_w_s, odd_b_s, odd_w_o, odd_norm_ffn, odd_router, odd_we_gate, odd_we_up, odd_we_down):
    bsz, seq, d = x.shape
    xf = x.reshape(bsz * seq, d)
    xf, h = _even_layer(xf, rel_bias, even_norm_mix[0], even_w_in[0], even_q_norm[0], even_k_norm[0],
                        even_conv_w[0], even_conv_b[0], even_cnorm_g[0], even_cnorm_b[0], even_w_out[0],
                        even_norm_ffn[0], bsz, seq)
    xf, h = _ffn(h, even_ffn_w1[0].astype(BF16), even_ffn_w3[0].astype(BF16), even_ffn_w2[0].astype(BF16),
                 xf, odd_norm_mix[0])
    z = _matmul(h, odd_w_u[0].astype(BF16), bias=odd_b_u[0], name="gmlp_in_proj")
    xf, h, logits = _gate_outproj(z, odd_w_s[0], odd_b_s[0], odd_vnorm_g[0], odd_vnorm_b[0],
                                  odd_w_o[0].astype(BF16), xf, odd_norm_ffn[0], odd_router[0])
    xf = _moe_layer(xf, h, logits, odd_we_gate[0], odd_we_up[0], odd_we_down[0])
    return xf.reshape(bsz, seq, d)
```

```python
import functools
import math

import numpy as np
import jax
import jax.numpy as jnp
from jax import lax
from jax.experimental import pallas as pl
from jax.experimental.pallas import tpu as pltpu

F32 = jnp.float32
BF16 = jnp.bfloat16
EPS = 1e-6

HEAD_DIM = 128
A_HEADS = 8
DILATED_BRANCHES = ((128, 1), (512, 4), (2048, 16))
ATT_BLK = 128
REL_BUCKETS = 32
REL_MAX_DIST = 2048
CONV_WIDTH = 31
CONV_HALO = 32
CHUNK = 128
GMLP_GROUPS = 16
N_EXPERTS = 8
NEG = -1e30

VMEM_LIMIT_BYTES = 56 * 1024 * 1024
MOE_TM = 1024
MOE_TF = 512
MOE_ROW_BLOCK = 256
SWIGLU_UP_CHUNK = 256
SWIGLU_DOWN_CHUNK = 512


def _cparams(sem):
    return pltpu.CompilerParams(dimension_semantics=sem, vmem_limit_bytes=VMEM_LIMIT_BYTES)


def _rms(x, g):
    ms = jnp.mean(x * x, axis=-1, keepdims=True)
    return x * lax.rsqrt(ms + EPS) * g


def _rmsnorm_kernel(x_ref, g_ref, o_ref):
    o_ref[...] = _rms(x_ref[...], g_ref[...]).astype(o_ref.dtype)


def _rmsnorm(x, g, tm=512):
    n, d = x.shape
    return pl.pallas_call(
        _rmsnorm_kernel,
        grid=(n // tm,),
        in_specs=[pl.BlockSpec((tm, d), lambda i: (i, 0)),
                  pl.BlockSpec((1, d), lambda i: (0, 0))],
        out_specs=pl.BlockSpec((tm, d), lambda i: (i, 0)),
        out_shape=jax.ShapeDtypeStruct((n, d), BF16),
        compiler_params=_cparams(("parallel",)),
        name="rmsnorm",
    )(x, g.reshape(1, d))


def _qkv_kernel(h_ref, w_ref, qg_ref, kg_ref, o_ref, *, tn, q_tiles, k_tiles, q_scale):
    j = pl.program_id(1)
    acc = jnp.dot(h_ref[...], w_ref[...], preferred_element_type=F32)

    def head_norm(g, scale):
        for hd in range(tn // HEAD_DIM):
            sl = slice(hd * HEAD_DIM, (hd + 1) * HEAD_DIM)
            o_ref[:, sl] = _rms(acc[:, sl], g) * scale

    @pl.when(j < q_tiles)
    def _():
        head_norm(qg_ref[...], q_scale)

    @pl.when(jnp.logical_and(j >= q_tiles, j < q_tiles + k_tiles))
    def _():
        head_norm(kg_ref[...], 1.0)

    @pl.when(j >= q_tiles + k_tiles)
    def _():
        o_ref[...] = acc


def _qkv_proj(h, w, q_g, k_g, a_width, tm=1024, tn=1024):
    n, d = h.shape
    nout = w.shape[1]
    kern = functools.partial(_qkv_kernel, tn=tn, q_tiles=a_width // tn, k_tiles=a_width // tn,
                             q_scale=HEAD_DIM ** -0.5)
    return pl.pallas_call(
        kern,
        grid=(n // tm, nout // tn),
        in_specs=[pl.BlockSpec((tm, d), lambda i, j: (i, 0)),
                  pl.BlockSpec((d, tn), lambda i, j: (0, j)),
                  pl.BlockSpec((1, HEAD_DIM), lambda i, j: (0, 0)),
                  pl.BlockSpec((1, HEAD_DIM), lambda i, j: (0, 0))],
        out_specs=pl.BlockSpec((tm, tn), lambda i, j: (i, j)),
        out_shape=jax.ShapeDtypeStruct((n, nout), F32),
        compiler_params=_cparams(("parallel", "parallel")),
        name="qkv_proj",
    )(h, w, q_g.reshape(1, HEAD_DIM), k_g.reshape(1, HEAD_DIM))


def _gelu_tanh(x):
    c = math.sqrt(2.0 / math.pi)
    return 0.5 * x * (1.0 + jnp.tanh(c * (x + 0.044715 * (x * x * x))))


def _mm_kernel(h_ref, w_ref, o_ref):
    o_ref[...] = jnp.dot(h_ref[...], w_ref[...], preferred_element_type=F32).astype(o_ref.dtype)


def _mm_bias_gelu_kernel(h_ref, w_ref, b_ref, o_ref):
    acc = jnp.dot(h_ref[...], w_ref[...], preferred_element_type=F32) + b_ref[...]
    o_ref[...] = _gelu_tanh(acc).astype(o_ref.dtype)


def _matmul(h, w, bias=None, tm=1024, tn=1024, name="matmul"):
    n, d = h.shape
    m = w.shape[1]
    in_specs = [pl.BlockSpec((tm, d), lambda i, j: (i, 0)),
                pl.BlockSpec((d, tn), lambda i, j: (0, j))]
    args = [h, w]
    kern = _mm_kernel
    if bias is not None:
        in_specs.append(pl.BlockSpec((1, tn), lambda i, j: (0, j)))
        args.append(bias.reshape(1, m))
        kern = _mm_bias_gelu_kernel
    return pl.pallas_call(
        kern,
        grid=(n // tm, m // tn),
        in_specs=in_specs,
        out_specs=pl.BlockSpec((tm, tn), lambda i, j: (i, j)),
        out_shape=jax.ShapeDtypeStruct((n, m), BF16),
        compiler_params=_cparams(("parallel", "parallel")),
        name=name,
    )(*args)


def _t5_bucket_np(dist):
    max_exact = REL_BUCKETS // 2
    d = np.maximum(dist, 0)
    log_ratio = (np.log(np.maximum(d, 1).astype(np.float32) / np.float32(max_exact))
                 / np.float32(math.log(REL_MAX_DIST / max_exact)))
    large = max_exact + (log_ratio.astype(np.float32) * np.float32(REL_BUCKETS - max_exact)).astype(np.int32)
    large = np.minimum(large, REL_BUCKETS - 1)
    return np.where(d < max_exact, d, large).astype(np.int32)


def _branch_bucket_tables():
    blk = ATT_BLK
    qi = np.arange(blk)[:, None]
    kj = np.arange(2 * blk)[None, :]
    dm = qi + blk - kj
    tabs = []
    for _, dil in DILATED_BRANCHES:
        bucket = _t5_bucket_np(dm * dil)
        tabs.append(np.where((dm >= 0) & (dm <= blk), bucket, -1))
    return np.stack(tabs).astype(np.int32)


def _attn_kernel(rb_ref, bidx_ref, q_ref, k_ref, v_ref, o_ref, bias_scr, ob_scr, m_scr, l_scr, *, seq):
    blk = ATT_BLK
    hidx = pl.program_id(0)

    @pl.when(pl.program_id(1) == 0)
    def _():
        for br in range(len(DILATED_BRANCHES)):
            idx = bidx_ref[br]
            bias = jnp.full(idx.shape, NEG, F32)
            for u in range(REL_BUCKETS):
                bias = jnp.where(idx == u, rb_ref[u, hidx], bias)
            bias_scr[br] = bias

    def block(br, q_rows, k_rows, bias):
        qb = q_ref[0, q_rows, :].astype(BF16)
        kb = k_ref[0, k_rows, :].astype(BF16)
        vb = v_ref[0, k_rows, :].astype(BF16)
        s = lax.dot_general(qb, kb, (((1,), (1,)), ((), ())), preferred_element_type=F32) + bias
        m = jnp.max(s, axis=-1, keepdims=True)
        p = jnp.exp(s - m)
        l = jnp.sum(p, axis=-1, keepdims=True)
        o = jnp.dot(p.astype(BF16), vb, preferred_element_type=F32)
        ob_scr[br, q_rows, :] = o
        m_scr[br, q_rows, :] = jnp.broadcast_to(m, (blk, HEAD_DIM))
        l_scr[br, q_rows, :] = jnp.broadcast_to(l, (blk, HEAD_DIM))

    for br, (window, dil) in enumerate(DILATED_BRANCHES):
        n_pos = seq // dil
        nb = -(-n_pos // blk)
        for r in range(dil):
            for nblk in range(nb):
                start = r + dil * blk * nblk

                def rows(first, count):
                    return pl.ds(first, count) if dil == 1 else pl.ds(first, count, stride=dil)

                q_rows = rows(start, blk)
                if nblk == 0:
                    block(br, q_rows, q_rows, bias_scr[br, :, blk:])
                else:
                    block(br, q_rows, rows(start - dil * blk, 2 * blk), bias_scr[br])

    n_br = len(DILATED_BRANCHES)
    m_all = m_scr[0]
    for br in range(1, n_br):
        m_all = jnp.maximum(m_all, m_scr[br])
    num = jnp.zeros((seq, HEAD_DIM), F32)
    den = jnp.zeros((seq, HEAD_DIM), F32)
    for br in range(n_br):
        a = jnp.exp(m_scr[br] - m_all)
        num = num + a * ob_scr[br]
        den = den + a * l_scr[br]
    o_ref[0] = (num / den).astype(o_ref.dtype)


def _attention(qkv, rel_bias, bsz, seq):
    h, e = A_HEADS, HEAD_DIM
    n_br = len(DILATED_BRANCHES)
    bidx = jnp.asarray(_branch_bucket_tables())
    kern = functools.partial(_attn_kernel, seq=seq)
    return pl.pallas_call(
        kern,
        grid=(h, bsz),
        in_specs=[pl.BlockSpec(memory_space=pltpu.SMEM),
                  pl.BlockSpec((n_br, ATT_BLK, 2 * ATT_BLK), lambda hh, b: (0, 0, 0)),
                  pl.BlockSpec((1, seq, e), lambda hh, b: (b, 0, hh)),
                  pl.BlockSpec((1, seq, e), lambda hh, b: (b, 0, h + hh)),
                  pl.BlockSpec((1, seq, e), lambda hh, b: (b, 0, 2 * h + hh))],
        out_specs=pl.BlockSpec((1, seq, e), lambda hh, b: (b, 0, hh)),
        out_shape=jax.ShapeDtypeStruct((bsz, seq, h * e), BF16),
        scratch_shapes=[pltpu.VMEM((n_br, ATT_BLK, 2 * ATT_BLK), F32),
                        pltpu.VMEM((n_br, seq, e), F32),
                        pltpu.VMEM((n_br, seq, e), F32),
                        pltpu.VMEM((n_br, seq, e), F32)],
        compiler_params=_cparams(("arbitrary", "arbitrary")),
        name="dilated_attention",
    )(rel_bias, bidx, qkv, qkv, qkv)


def _conv_kernel(cv_ref, cg_ref, hv_ref, hg_ref, w_ref, b_ref, lg_ref, lb_ref, o_ref, g_scr, y_scr, *, tt):
    halo = CONV_HALO
    t = pl.program_id(1)
    hv = hv_ref[0].astype(F32)
    hg = hg_ref[0].astype(F32)
    g_scr[0:halo, :] = jnp.where(t > 0, hv * jax.nn.sigmoid(hg), 0.0)
    cv = cv_ref[0].astype(F32)
    cg = cg_ref[0].astype(F32)
    g_scr[halo:, :] = cv * jax.nn.sigmoid(cg)

    ch = g_scr.shape[1]
    rc, cc = 32, 256
    first = halo - (CONV_WIDTH - 1)
    for r0 in range(0, tt, rc):
        for c0 in range(0, ch, cc):
            acc = jnp.broadcast_to(b_ref[:, c0:c0 + cc], (rc, cc))
            for k in range(CONV_WIDTH):
                acc = acc + w_ref[k:k + 1, c0:c0 + cc] * g_scr[r0 + first + k:r0 + first + k + rc, c0:c0 + cc]
            y_scr[r0:r0 + rc, c0:c0 + cc] = acc

    y = y_scr[...]
    mu = jnp.mean(y, axis=-1, keepdims=True)
    yc = y - mu
    var = jnp.mean(yc * yc, axis=-1, keepdims=True)
    z = yc * lax.rsqrt(var + EPS) * lg_ref[...] + lb_ref[...]
    o_ref[0] = (z * jax.nn.sigmoid(z)).astype(o_ref.dtype)


def _conv_module(pc, conv_w, conv_b, ln_g, ln_b, tt=256):
    bsz, seq, ch2 = pc.shape
    ch = ch2 // 2
    hb = tt // CONV_HALO
    kern = functools.partial(_conv_kernel, tt=tt)
    vec = lambda a: a.reshape(1, ch)
    return pl.pallas_call(
        kern,
        grid=(bsz, seq // tt),
        in_specs=[pl.BlockSpec((1, tt, ch), lambda b, t: (b, t, 0)),
                  pl.BlockSpec((1, tt, ch), lambda b, t: (b, t, 1)),
                  pl.BlockSpec((1, CONV_HALO, ch), lambda b, t: (b, jnp.maximum(t * hb - 1, 0), 0)),
                  pl.BlockSpec((1, CONV_HALO, ch), lambda b, t: (b, jnp.maximum(t * hb - 1, 0), 1)),
                  pl.BlockSpec((CONV_WIDTH, ch), lambda b, t: (0, 0)),
                  pl.BlockSpec((1, ch), lambda b, t: (0, 0)),
                  pl.BlockSpec((1, ch), lambda b, t: (0, 0)),
                  pl.BlockSpec((1, ch), lambda b, t: (0, 0))],
        out_specs=pl.BlockSpec((1, tt, ch), lambda b, t: (b, t, 0)),
        out_shape=jax.ShapeDtypeStruct((bsz, seq, ch), BF16),
        scratch_shapes=[pltpu.VMEM((tt + CONV_HALO, ch), F32), pltpu.VMEM((tt, ch), F32)],
        compiler_params=_cparams(("parallel", "parallel")),
        name="conv_module",
    )(pc, pc, pc, pc, conv_w, vec(conv_b), vec(ln_g), vec(ln_b))


def _outproj_kernel(a_ref, b_ref, wa_ref, wb_ref, x_ref, g_ref, xo_ref, ho_ref):
    acc = jnp.dot(a_ref[...], wa_ref[...], preferred_element_type=F32)
    acc = acc + jnp.dot(b_ref[...], wb_ref[...], preferred_element_type=F32)
    xn = x_ref[...] + acc
    xo_ref[...] = xn
    ho_ref[...] = _rms(xn, g_ref[...]).astype(ho_ref.dtype)


def _outproj(a, b, wa, wb, x, g, tm=512):
    n, d = x.shape
    ka, kb = a.shape[1], b.shape[1]
    return pl.pallas_call(
        _outproj_kernel,
        grid=(n // tm,),
        in_specs=[pl.BlockSpec((tm, ka), lambda i: (i, 0)),
                  pl.BlockSpec((tm, kb), lambda i: (i, 0)),
                  pl.BlockSpec((ka, d), lambda i: (0, 0)),
                  pl.BlockSpec((kb, d), lambda i: (0, 0)),
                  pl.BlockSpec((tm, d), lambda i: (i, 0)),
                  pl.BlockSpec((1, d), lambda i: (0, 0))],
        out_specs=[pl.BlockSpec((tm, d), lambda i: (i, 0)),
                   pl.BlockSpec((tm, d), lambda i: (i, 0))],
        out_shape=[jax.ShapeDtypeStruct((n, d), F32), jax.ShapeDtypeStruct((n, d), BF16)],
        compiler_params=_cparams(("parallel",)),
        name="out_proj",
    )(a, b, wa, wb, x, g.reshape(1, d))


def _swiglu_step(x, load_gate, load_up, load_down, tf, acc_ref, rows=slice(None)):
    hid = []
    for c0 in range(0, tf, SWIGLU_UP_CHUNK):
        cols = slice(c0, c0 + SWIGLU_UP_CHUNK)
        a = jnp.dot(x, load_gate(cols), preferred_element_type=F32)
        b = jnp.dot(x, load_up(cols), preferred_element_type=F32)
        hid.append((a * jax.nn.sigmoid(a) * b).astype(BF16))
    hid = jnp.concatenate(hid, axis=1)
    for n0 in range(0, acc_ref.shape[1], SWIGLU_DOWN_CHUNK):
        cols = slice(n0, n0 + SWIGLU_DOWN_CHUNK)
        acc_ref[rows, cols] += jnp.dot(hid, load_down(cols), preferred_element_type=F32)


def _ffn_kernel(h_ref, w1_ref, w3_ref, w2_ref, x_ref, g_ref, xo_ref, ho_ref, *, tf):
    f = pl.program_id(1)

    @pl.when(f == 0)
    def _():
        xo_ref[...] = jnp.zeros_like(xo_ref)

    _swiglu_step(h_ref[...], lambda c: w1_ref[:, c], lambda c: w3_ref[:, c], lambda c: w2_ref[:, c],
                 tf, xo_ref)

    @pl.when(f == pl.num_programs(1) - 1)
    def _():
        xn = x_ref[...] + xo_ref[...]
        xo_ref[...] = xn
        ho_ref[...] = _rms(xn, g_ref[...]).astype(ho_ref.dtype)


def _ffn(h, w1, w3, w2, x, g, tm=512, tf=512):
    n, d = x.shape
    ff = w1.shape[1]
    return pl.pallas_call(
        functools.partial(_ffn_kernel, tf=tf),
        grid=(n // tm, ff // tf),
        in_specs=[pl.BlockSpec((tm, d), lambda i, f: (i, 0)),
                  pl.BlockSpec((d, tf), lambda i, f: (0, f)),
                  pl.BlockSpec((d, tf), lambda i, f: (0, f)),
                  pl.BlockSpec((tf, d), lambda i, f: (f, 0)),
                  pl.BlockSpec((tm, d), lambda i, f: (i, 0)),
                  pl.BlockSpec((1, d), lambda i, f: (0, 0))],
        out_specs=[pl.BlockSpec((tm, d), lambda i, f: (i, 0)),
                   pl.BlockSpec((tm, d), lambda i, f: (i, 0))],
        out_shape=[jax.ShapeDtypeStruct((n, d), F32), jax.ShapeDtypeStruct((n, d), BF16)],
        compiler_params=_cparams(("parallel", "arbitrary")),
        name="dense_swiglu",
    )(h, w1, w3, w2, x, g.reshape(1, d))


def _gate_kernel(z_ref, ws_ref, bs_ref, vg_ref, vb_ref, wo_ref, x_ref, g_ref, wr_ref,
                 xo_ref, ho_ref, lg_ref, gated_scr, *, tm, width):
    gch = width // GMLP_GROUPS
    row = lax.broadcasted_iota(jnp.int32, (CHUNK, CHUNK), 0)
    col = lax.broadcasted_iota(jnp.int32, (CHUNK, CHUNK), 1)
    causal = row >= col
    for c0 in range(0, tm, CHUNK):
        v = z_ref[c0:c0 + CHUNK, width:].astype(F32)
        mu = jnp.mean(v, axis=-1, keepdims=True)
        vc = v - mu
        var = jnp.mean(vc * vc, axis=-1, keepdims=True)
        vn = (vc * lax.rsqrt(var + EPS) * vg_ref[...] + vb_ref[...]).astype(BF16)
        for gi in range(GMLP_GROUPS):
            cs = slice(gi * gch, (gi + 1) * gch)
            wsg = jnp.where(causal, ws_ref[gi], 0.0).astype(BF16)
            sv = jnp.dot(wsg, vn[:, cs], preferred_element_type=F32) + bs_ref[:, cs]
            u = z_ref[c0:c0 + CHUNK, cs].astype(F32)
            gated_scr[c0:c0 + CHUNK, cs] = (u * sv).astype(BF16)
    xn = x_ref[...] + jnp.dot(gated_scr[...], wo_ref[...], preferred_element_type=F32)
    xo_ref[...] = xn
    hn = _rms(xn, g_ref[...])
    ho_ref[...] = hn.astype(ho_ref.dtype)
    lg_ref[...] = jnp.dot(hn, wr_ref[...], preferred_element_type=F32, precision=lax.Precision.HIGHEST)


def _gate_outproj(z, w_s, b_s, vn_g, vn_b, w_o, x, g, w_router, tm=512):
    n, d = x.shape
    width = z.shape[1] // 2
    gch = width // GMLP_GROUPS
    bs_x = jnp.repeat(b_s.T, gch, axis=1)
    wr = jnp.zeros((d, 128), F32).at[:, :N_EXPERTS].set(w_router)
    kern = functools.partial(_gate_kernel, tm=tm, width=width)
    full = lambda shape: pl.BlockSpec(shape, lambda i: (0,) * len(shape))
    return pl.pallas_call(
        kern,
        grid=(n // tm,),
        in_specs=[pl.BlockSpec((tm, 2 * width), lambda i: (i, 0)),
                  full((GMLP_GROUPS, CHUNK, CHUNK)),
                  full((CHUNK, width)),
                  full((1, width)),
                  full((1, width)),
                  full((width, d)),
                  pl.BlockSpec((tm, d), lambda i: (i, 0)),
                  full((1, d)),
                  full((d, 128))],
        out_specs=[pl.BlockSpec((tm, d), lambda i: (i, 0)),
                   pl.BlockSpec((tm, d), lambda i: (i, 0)),
                   pl.BlockSpec((tm, 128), lambda i: (i, 0))],
        out_shape=[jax.ShapeDtypeStruct((n, d), F32), jax.ShapeDtypeStruct((n, d), F32),
                   jax.ShapeDtypeStruct((n, 128), F32)],
        scratch_shapes=[pltpu.VMEM((tm, width), BF16)],
        compiler_params=_cparams(("parallel",)),
        name="gmlp_gate_outproj",
    )(z, w_s, bs_x, vn_g.reshape(1, width), vn_b.reshape(1, width), w_o, x, g.reshape(1, d), wr)


def _router_kernel(lg_ref, oi_ref, og_ref, cnt_ref, carry_scr, *, tm):
    i = pl.program_id(0)

    @pl.when(i == 0)
    def _():
        carry_scr[...] = jnp.zeros_like(carry_scr)

    lane = lax.broadcasted_iota(jnp.int32, (tm, 128), 1)
    lg = jnp.where(lane < N_EXPERTS, lg_ref[...], -jnp.inf)
    m1 = jnp.max(lg, axis=-1, keepdims=True)
    i1 = jnp.min(jnp.where(lg == m1, lane, 128), axis=-1, keepdims=True)
    lg2 = jnp.where(lane == i1, -jnp.inf, lg)
    m2 = jnp.max(lg2, axis=-1, keepdims=True)
    i2 = jnp.min(jnp.where(lg2 == m2, lane, 128), axis=-1, keepdims=True)
    e2 = jnp.exp(m2 - m1)
    den = 1.0 + e2
    g1 = 1.0 / den
    g2 = e2 / den

    sel = jnp.logical_or(lane == i1, lane == i2)
    row = lax.broadcasted_iota(jnp.int32, (tm, tm), 0)
    col = lax.broadcasted_iota(jnp.int32, (tm, tm), 1)
    before = jnp.where(row > col, 1.0, 0.0).astype(BF16)
    selb = jnp.where(sel, 1.0, 0.0)
    tot = jnp.dot(before, selb.astype(BF16), preferred_element_type=F32) + carry_scr[0:1, :]
    r1 = jnp.sum(jnp.where(lane == i1, tot, 0.0), axis=-1, keepdims=True).astype(jnp.int32)
    r2 = jnp.sum(jnp.where(lane == i2, tot, 0.0), axis=-1, keepdims=True).astype(jnp.int32)
    new_carry = carry_scr[0:1, :] + jnp.sum(selb, axis=0, keepdims=True)
    carry_scr[...] = jnp.broadcast_to(new_carry, carry_scr.shape)
    cnt_ref[...] = jnp.broadcast_to(new_carry, cnt_ref.shape)

    oi_ref[...] = jnp.where(lane == 0, i1, jnp.where(lane == 1, i2, jnp.where(lane == 2, r1, r2)))
    og_ref[...] = jnp.where(lane == 0, g1, g2)


def _router(logits, tm=512):
    n = logits.shape[0]
    kern = functools.partial(_router_kernel, tm=tm)
    return pl.pallas_call(
        kern,
        grid=(n // tm,),
        in_specs=[pl.BlockSpec((tm, 128), lambda i: (i, 0))],
        out_specs=[pl.BlockSpec((tm, 128), lambda i: (i, 0)),
                   pl.BlockSpec((tm, 128), lambda i: (i, 0)),
                   pl.BlockSpec((8, 128), lambda i: (0, 0))],
        out_shape=[jax.ShapeDtypeStruct((n, 128), jnp.int32), jax.ShapeDtypeStruct((n, 128), F32),
                   jax.ShapeDtypeStruct((8, 128), F32)],
        scratch_shapes=[pltpu.VMEM((8, 128), F32)],
        compiler_params=_cparams(("arbitrary",)),
        name="router_top2",
    )(logits)


def _gather_rows(src_ref, idx_ref, base, buf, sem, tt):
    def copy(r):
        return pltpu.make_async_copy(src_ref.at[pl.ds(idx_ref[base + r], 1)], buf.at[pl.ds(r, 1)], sem)

    def start(r, carry):
        copy(r).start()
        return carry

    def wait(r, carry):
        copy(r).wait()
        return carry

    lax.fori_loop(0, tt, start, 0, unroll=8)
    lax.fori_loop(0, tt, wait, 0, unroll=8)


def _dispatch_kernel(tok_ref, nt_ref, h_ref, xs_ref, buf, sem, *, tt):
    i = pl.program_id(0)

    @pl.when(i < nt_ref[0])
    def _():
        _gather_rows(h_ref, tok_ref, i * tt, buf, sem, tt)
        xs_ref[...] = buf[...].astype(xs_ref.dtype)

    @pl.when(i >= nt_ref[0])
    def _():
        xs_ref[...] = jnp.zeros_like(xs_ref)


def _dispatch(h, row_token, n_live, tt):
    n, d = h.shape
    n_rows = row_token.shape[0]
    kern = functools.partial(_dispatch_kernel, tt=tt)
    return pl.pallas_call(
        kern,
        grid_spec=pltpu.PrefetchScalarGridSpec(
            num_scalar_prefetch=2,
            grid=(n_rows // tt,),
            in_specs=[pl.BlockSpec(memory_space=pl.ANY)],
            out_specs=pl.BlockSpec((tt, d), lambda i, tok, nt: (i, 0)),
            scratch_shapes=[pltpu.VMEM((tt, d), F32), pltpu.SemaphoreType.DMA(())]),
        out_shape=jax.ShapeDtypeStruct((n_rows, d), BF16),
        compiler_params=_cparams(("arbitrary",)),
        name="moe_dispatch",
    )(row_token, n_live, h)


def _moe_kernel(te_ref, nt_ref, tb_ref, xs_ref, wg_ref, wu_ref, wd_ref, ys_ref, *, tf, tm):
    del te_ref, nt_ref
    i = pl.program_id(0)
    f = pl.program_id(1)

    @pl.when(f == 0)
    def _():
        ys_ref[...] = jnp.zeros_like(ys_ref)

    for k in range(1, tm // MOE_ROW_BLOCK + 1):
        @pl.when(tb_ref[i] == k)
        def _(k=k):
            rows = slice(0, k * MOE_ROW_BLOCK)
            _swiglu_step(xs_ref[rows, :],
                         lambda c: wg_ref[0, :, c].astype(BF16),
                         lambda c: wu_ref[0, :, c].astype(BF16),
                         lambda c: wd_ref[0, :, c].astype(BF16),
                         tf, ys_ref, rows)


def _moe_experts(xs, tile_expert, n_live, tile_blocks, w_gate, w_up, w_down, tm=MOE_TM, tf=MOE_TF):
    n_rows, d = xs.shape
    ff = w_gate.shape[2]
    n_tiles = n_rows // tm
    n_f = ff // tf

    def live_tile(i, nt):
        return jnp.minimum(i, nt[0] - 1)

    def live_f(i, f, nt):
        return jnp.where(i < nt[0], f, n_f - 1)

    return pl.pallas_call(
        functools.partial(_moe_kernel, tf=tf, tm=tm),
        grid_spec=pltpu.PrefetchScalarGridSpec(
            num_scalar_prefetch=3,
            grid=(n_tiles, n_f),
            in_specs=[pl.BlockSpec((tm, d), lambda i, f, te, nt, tb: (live_tile(i, nt), 0)),
                      pl.BlockSpec((1, d, tf), lambda i, f, te, nt, tb: (te[i], 0, live_f(i, f, nt))),
                      pl.BlockSpec((1, d, tf), lambda i, f, te, nt, tb: (te[i], 0, live_f(i, f, nt))),
                      pl.BlockSpec((1, tf, d), lambda i, f, te, nt, tb: (te[i], live_f(i, f, nt), 0))],
            out_specs=pl.BlockSpec((tm, d), lambda i, f, te, nt, tb: (i, 0))),
        out_shape=jax.ShapeDtypeStruct((n_rows, d), F32),
        compiler_params=_cparams(("arbitrary", "arbitrary")),
        name="moe_experts",
    )(tile_expert, n_live, tile_blocks, xs, w_gate, w_up, w_down)


def _combine_kernel(p1_ref, p2_ref, ys_ref, x_ref, g_ref, o_ref, buf1, buf2, sem, *, tt):
    base = pl.program_id(0) * tt

    def copies(t):
        return (pltpu.make_async_copy(ys_ref.at[pl.ds(p1_ref[base + t], 1)], buf1.at[pl.ds(t, 1)], sem),
                pltpu.make_async_copy(ys_ref.at[pl.ds(p2_ref[base + t], 1)], buf2.at[pl.ds(t, 1)], sem))

    def start(t, carry):
        for c in copies(t):
            c.start()
        return carry

    def wait(t, carry):
        for c in copies(t):
            c.wait()
        return carry

    lax.fori_loop(0, tt, start, 0, unroll=8)
    lax.fori_loop(0, tt, wait, 0, unroll=8)
    g = g_ref[...]
    o_ref[...] = x_ref[...] + (buf1[...] * g[:, 0:1] + buf2[...] * g[:, 1:2])


def _combine(ys, pos1, pos2, gates, x, tt=256):
    n, d = x.shape
    kern = functools.partial(_combine_kernel, tt=tt)
    return pl.pallas_call(
        kern,
        grid_spec=pltpu.PrefetchScalarGridSpec(
            num_scalar_prefetch=2,
            grid=(n // tt,),
            in_specs=[pl.BlockSpec(memory_space=pl.ANY),
                      pl.BlockSpec((tt, d), lambda i, p1, p2: (i, 0)),
                      pl.BlockSpec((tt, 128), lambda i, p1, p2: (i, 0))],
            out_specs=pl.BlockSpec((tt, d), lambda i, p1, p2: (i, 0)),
            scratch_shapes=[pltpu.VMEM((tt, d), F32), pltpu.VMEM((tt, d), F32),
                            pltpu.SemaphoreType.DMA(())]),
        out_shape=jax.ShapeDtypeStruct((n, d), F32),
        compiler_params=_cparams(("arbitrary",)),
        name="moe_combine",
    )(pos1, pos2, ys, x, gates)


def _even_layer(x, rel_bias, norm_mix, w_in, q_g, k_g, conv_w, conv_b, cn_g, cn_b, w_out, norm_next,
                bsz, seq):
    n, d = x.shape
    a_width = A_HEADS * HEAD_DIM
    h = _rmsnorm(x, norm_mix)
    w_in = w_in.astype(BF16)
    qkv = _qkv_proj(h, w_in[:, :3 * a_width], q_g, k_g, a_width)
    pc = _matmul(h, w_in[:, 3 * a_width:], name="conv_proj")
    ch = pc.shape[1] // 2
    o_a = _attention(qkv.reshape(bsz, seq, 3 * a_width), rel_bias, bsz, seq).reshape(n, a_width)
    o_b = _conv_module(pc.reshape(bsz, seq, 2 * ch), conv_w, conv_b, cn_g, cn_b).reshape(n, ch)
    w_out = w_out.astype(BF16)
    return _outproj(o_a, o_b, w_out[:a_width], w_out[a_width:], x, norm_next)


def _moe_layer(x, h, logits, w_gate, w_up, w_down):
    n, d = x.shape
    tm = MOE_TM
    oi, og, cnt = _router(logits)
    e1, e2, r1, r2 = oi[:, 0], oi[:, 1], oi[:, 2], oi[:, 3]
    counts = cnt[0, :N_EXPERTS].astype(jnp.int32)
    padded = ((counts + tm - 1) // tm) * tm
    ends = jnp.cumsum(padded)
    starts = ends - padded
    pos1 = starts[e1] + r1
    pos2 = starts[e2] + r2
    n_tiles = (2 * n) // tm + N_EXPERTS
    n_live = (ends[-1] // tm).astype(jnp.int32).reshape(1)
    tile_row = jnp.minimum(jnp.arange(n_tiles, dtype=jnp.int32), n_live[0] - 1) * tm
    tile_expert = jnp.sum(tile_row[:, None] >= ends[None, :], axis=1).astype(jnp.int32)
    tok = jnp.arange(n, dtype=jnp.int32)
    row_token = jnp.zeros((n_tiles * tm,), jnp.int32).at[jnp.concatenate([pos1, pos2])].set(
        jnp.concatenate([tok, tok]))
    tile_live = jnp.arange(n_tiles, dtype=jnp.int32) < n_live[0]
    tile_rows = jnp.clip((starts + counts)[tile_expert] - tile_row, 0, tm)
    tile_blocks = jnp.where(tile_live, (tile_rows + MOE_ROW_BLOCK - 1) // MOE_ROW_BLOCK, 0).astype(jnp.int32)
    xs = _dispatch(h, row_token, n_live, tm)
    ys = _moe_experts(xs, tile_expert, n_live, tile_blocks, w_gate, w_up, w_down)
    return _combine(ys, pos1, pos2, og, x)


def kernel(x, rel_bias, even_norm_mix, even_w_in, even_q_norm, even_k_norm, even_conv_w, even_conv_b, even_cnorm_g, even_cnorm_b, even_w_out, even_norm_ffn, even_ffn_w1, even_ffn_w3, even_ffn_w2, odd_norm_mix, odd_w_u, odd_b_u, odd_vnorm_g, odd_vnorm_b, odd_w_s, odd_b_s, odd_w_o, odd_norm_ffn, odd_router, odd_we_gate, odd_we_up, odd_we_down):
    bsz, seq, d = x.shape
    xf = x.reshape(bsz * seq, d)
    xf, h = _even_layer(xf, rel_bias, even_norm_mix[0], even_w_in[0], even_q_norm[0], even_k_norm[0],
                        even_conv_w[0], even_conv_b[0], even_cnorm_g[0], even_cnorm_b[0], even_w_out[0],
                        even_norm_ffn[0], bsz, seq)
    xf, h = _ffn(h, even_ffn_w1[0].astype(BF16), even_ffn_w3[0].astype(BF16), even_ffn_w2[0].astype(BF16),
                 xf, odd_norm_mix[0])
    z = _matmul(h, odd_w_u[0].astype(BF16), bias=odd_b_u[0], name="gmlp_in_proj")
    xf, h, logits = _gate_outproj(z, odd_w_s[0], odd_b_s[0], odd_vnorm_g[0], odd_vnorm_b[0],
                                  odd_w_o[0].astype(BF16), xf, odd_norm_ffn[0], odd_router[0])
    xf = _moe_layer(xf, h, logits, odd_we_gate[0], odd_we_up[0], odd_we_down[0])
    return xf.reshape(bsz, seq, d)
```
